```python
import jax, jax.numpy as jnp
from jax import lax
import numpy as np

D_MODEL = 2048
BATCH = 4
SEQ = 2048
DEPTH = 2
DEC_BATCH = 8
DEC_SEQ = 16
PAST_LEN = 1024

CHUNK = 64
QBLOCK = 128
HEAD_DIM = 128
H_A = D_MODEL // (2 * HEAD_DIM)
W_A = H_A * HEAD_DIM
H_B = D_MODEL // (2 * HEAD_DIM)
DK_B = 128
DV_B = 128
W_B = H_B * DV_B
H_C = D_MODEL // HEAD_DIM
W_C = H_C * HEAD_DIM
H_X = 4
HD_X = D_MODEL // H_X
N_MEM = 256
D_FF = ((8 * D_MODEL // 3 + 127) // 128) * 128
N_EVEN = (DEPTH + 1) // 2
N_ODD = DEPTH // 2
D_IN_EVEN = 3 * W_A + H_A + 4 * W_B
D_IN_ODD = 3 * W_C
ALPHA = (2.0 * DEPTH) ** 0.25
BETA = (8.0 * DEPTH) ** -0.25
LN_EPS = 1e-5
RMS_EPS = 1e-6
F32 = jnp.float32

kernel_name = 'hybrid_streaming_encoder_step'


def layer_norm(x, g, b):
    mu = jnp.mean(x, axis=-1, keepdims=True)
    var = jnp.mean(jnp.square(x - mu), axis=-1, keepdims=True)
    return (x - mu) * lax.rsqrt(var + LN_EPS) * g.astype(F32) + b.astype(F32)


def post_norm(x, sub, g, b):
    return layer_norm(ALPHA * x.astype(F32) + sub.astype(F32), g, b).astype(x.dtype)


def swiglu_half(x, w_gate, w_up, w_down):
    return 0.5 * ((jax.nn.silu(x @ w_gate) * (x @ w_up)) @ w_down)


def fox_core(q, k, v, cum_q, cum_k, qpos, kpos):
    logits = jnp.einsum('bqhd,bkhd->bhqk', q, k).astype(F32) * (HEAD_DIM ** -0.5)
    bias = jnp.swapaxes(cum_q, 1, 2)[:, :, :, None] - jnp.swapaxes(cum_k, 1, 2)[:, :, None, :]
    mask = qpos[:, None] >= kpos[None, :]
    p = jax.nn.softmax(jnp.where(mask, logits + bias, -jnp.inf), axis=-1)
    return jnp.einsum('bhqk,bkhd->bqhd', p.astype(v.dtype), v)


def fox_prompt(q, k, v, logf):
    B, S, H, D = q.shape
    nb = S // QBLOCK
    cum = jnp.cumsum(logf, axis=1)
    pos = jnp.arange(S)
    qb = q.reshape(B, nb, QBLOCK, H, D).swapaxes(0, 1)
    cb = cum.reshape(B, nb, QBLOCK, H).swapaxes(0, 1)
    pb = pos.reshape(nb, QBLOCK)
    out = lax.map(lambda a: fox_core(a[0], k, v, a[1], cum, a[2], pos), (qb, cb, pb))
    return out.swapaxes(0, 1).reshape(B, S, H * D)


def sb_core(q, k, v, qpos, kpos):
    z = jnp.einsum('bqhd,bkhd->bhqk', q, k).astype(F32) * (HEAD_DIM ** -0.5)
    mask = kpos[None, :] < qpos[:, None]
    log_1m = jnp.where(mask, jax.nn.log_sigmoid(-z), 0.0)
    rem = lax.cumsum(log_1m, axis=3, reverse=True) - log_1m
    w = jnp.where(mask, jnp.exp(jax.nn.log_sigmoid(z) + rem), 0.0)
    return jnp.einsum('bhqk,bkhd->bqhd', w.astype(v.dtype), v)


def sb_prompt(q, k, v):
    B, S, H, D = q.shape
    nb = S // QBLOCK
    pos = jnp.arange(S)
    qb = q.reshape(B, nb, QBLOCK, H, D).swapaxes(0, 1)
    pb = pos.reshape(nb, QBLOCK)
    out = lax.map(lambda a: sb_core(a[0], k, v, a[1], pos), (qb, pb))
    return out.swapaxes(0, 1).reshape(B, S, H * D)


def hgrn_chunk(s0, q, logf, k, i):
    L = q.shape[1]
    b = jnp.cumsum(logf, axis=1)
    causal = (jnp.arange(L)[:, None] >= jnp.arange(L)[None, :])[None, :, :, None, None]
    decay = jnp.exp(jnp.where(causal, b[:, :, None] - b[:, None, :], -jnp.inf))
    scores = jnp.einsum('bthc,btshc,bshc->bhts', q, decay, k)
    o = jnp.einsum('bhts,bshv->bthv', scores, i) + jnp.einsum('bthc,bhcv->bthv', q * jnp.exp(b), s0)
    b_last = b[:, -1]
    s_new = jnp.exp(b_last)[..., None] * s0 + jnp.einsum('bshc,bshv->bhcv', k * jnp.exp(b_last[:, None] - b), i)
    return o, s_new


def hgrn_prompt(q, logf, k, i):
    B, S, H, DK = q.shape
    DV = i.shape[-1]
    n = S // CHUNK
    to_blocks = lambda t: t.reshape(B, n, CHUNK, H, t.shape[-1]).swapaxes(0, 1)

    def step(state, xs):
        o, state = hgrn_chunk(state, *xs)
        return state, o

    s0 = jnp.zeros((B, H, DK, DV), F32)
    s_fin, o = lax.scan(step, s0, (to_blocks(q), to_blocks(logf), to_blocks(k), to_blocks(i)))
    return o.swapaxes(0, 1).reshape(B, S, H, DV), s_fin


def hgrn_lower_bound(lb_logits, j):
    return jnp.cumsum(jax.nn.softmax(lb_logits.astype(F32), axis=0), axis=0)[j]


def even_project(x, w_in, b_f, lb):
    B, T, _ = x.shape
    h = x @ w_in
    offs = np.cumsum([W_A, W_A, W_A, H_A, W_B, W_B, W_B, W_B])[:-1].tolist()
    qa, ka, va, fa, qb, fb, ib, gb = jnp.split(h, offs, axis=-1)
    qa = qa.reshape(B, T, H_A, HEAD_DIM)
    ka = ka.reshape(B, T, H_A, HEAD_DIM)
    va = va.reshape(B, T, H_A, HEAD_DIM)
    logf_a = jax.nn.log_sigmoid(fa.astype(F32) + b_f.astype(F32))
    f_b = lb + (1.0 - lb) * jax.nn.sigmoid(fb.astype(F32))
    logf_b = jnp.log(f_b).reshape(B, T, H_B, DK_B)
    k_b = (1.0 - f_b).reshape(B, T, H_B, DK_B)
    q_b = jax.nn.silu(qb.astype(F32)).reshape(B, T, H_B, DK_B)
    i_b = ib.astype(F32).reshape(B, T, H_B, DV_B)
    return (qa, ka, va, logf_a), (q_b, logf_b, k_b, i_b, gb)


def even_out(o_a, o_b, g_b, norm_g, w_out):
    B, T = o_a.shape[:2]
    ob = o_b * lax.rsqrt(jnp.mean(jnp.square(o_b), axis=-1, keepdims=True) + RMS_EPS)
    ob = ob.reshape(B, T, W_B) * norm_g.astype(F32) * jax.nn.silu(g_b.astype(F32))
    merged = jnp.concatenate([o_a, ob.astype(o_a.dtype)], axis=-1)
    return merged @ w_out


def even_mixer_prompt(x, w_in, b_f, lb, norm_g, w_out):
    (qa, ka, va, lfa), (qb, lfb, kb, ib, gb) = even_project(x, w_in, b_f, lb)
    oa = fox_prompt(qa, ka, va, lfa)
    ob, s_fin = hgrn_prompt(qb, lfb, kb, ib)
    y = even_out(oa, ob, gb, norm_g, w_out)
    return y, ka, va, lfa.astype(x.dtype), s_fin.astype(x.dtype)


def even_mixer_sample(x, c_k, c_v, c_logf, s0, w_in, b_f, lb, norm_g, w_out):
    (qa, ka, va, lfa), (qb, lfb, kb, ib, gb) = even_project(x, w_in, b_f, lb)
    B, T = x.shape[:2]
    P = c_k.shape[1]
    k_all = jnp.concatenate([c_k.astype(ka.dtype), ka], axis=1)
    v_all = jnp.concatenate([c_v.astype(va.dtype), va], axis=1)
    cum = jnp.cumsum(jnp.concatenate([c_logf.astype(F32), lfa], axis=1), axis=1)
    pos = jnp.arange(P + T)
    oa = fox_core(qa, k_all, v_all, cum[:, P:], cum, pos[P:], pos).reshape(B, T, W_A)
    ob, s_new = hgrn_chunk(s0.astype(F32), qb, lfb, kb, ib)
    y = even_out(oa, ob, gb, norm_g, w_out)
    return y, ka, va, lfa.astype(x.dtype), s_new.astype(x.dtype)


def odd_project(x, w_in):
    B, T, _ = x.shape
    q, k, v = jnp.split(x @ w_in, 3, axis=-1)
    r = lambda t: t.reshape(B, T, H_C, HEAD_DIM)
    return r(q), r(k), r(v)


def odd_mixer_prompt(x, w_in, w_out):
    q, k, v = odd_project(x, w_in)
    return sb_prompt(q, k, v) @ w_out, k, v


def odd_mixer_sample(x, c_k, c_v, w_in, w_out):
    q, k, v = odd_project(x, w_in)
    B, T = x.shape[:2]
    P = c_k.shape[1]
    k_all = jnp.concatenate([c_k.astype(k.dtype), k], axis=1)
    v_all = jnp.concatenate([c_v.astype(v.dtype), v], axis=1)
    pos = jnp.arange(P + T)
    o = sb_core(q, k_all, v_all, pos[P:], pos).reshape(B, T, W_C)
    return o @ w_out, k, v


def mem_kv(mem, w_kv):
    B, N, _ = mem.shape
    mk, mv = jnp.split(mem @ w_kv, 2, axis=-1)
    return mk.reshape(B, N, H_X, HD_X), mv.reshape(B, N, H_X, HD_X)


def cross_attend(x, mk, mv, w_q, w_o):
    B, T, _ = x.shape
    q = (x @ w_q).reshape(B, T, H_X, HD_X)
    logits = jnp.einsum('bthd,bmhd->bhtm', q, mk.astype(q.dtype)).astype(F32) * (HD_X ** -0.5)
    p = jax.nn.softmax(logits, axis=-1)
    o = jnp.einsum('bhtm,bmhd->bthd', p.astype(x.dtype), mv.astype(x.dtype)).reshape(B, T, D_MODEL)
    return o @ w_o


def setup_inputs(seed: int = 0) -> dict:
    key = jax.random.key(seed)
    ks = list(jax.random.split(key, 32))
    cnt = [0]

    def nrm(shape, scale):
        k = ks[cnt[0]]
        cnt[0] += 1
        return jax.random.normal(k, shape, F32) * scale

    d_in = float(D_MODEL) ** -0.5
    return {
        'x_prompt': nrm((BATCH, SEQ, D_MODEL), 1.0),
        'x_sample': nrm((DEC_BATCH, DEC_SEQ, D_MODEL), 1.0),
        'mem_prompt': nrm((BATCH, N_MEM, D_MODEL), 1.0),
        'cache_fox_k': nrm((N_EVEN, DEC_BATCH, PAST_LEN, H_A, HEAD_DIM), 1.0),
        'cache_fox_v': nrm((N_EVEN, DEC_BATCH, PAST_LEN, H_A, HEAD_DIM), 1.0),
        'cache_fox_logf': jax.nn.log_sigmoid(2.0 + nrm((N_EVEN, DEC_BATCH, PAST_LEN, H_A), 1.0)),
        'state_hgrn': nrm((N_EVEN, DEC_BATCH, H_B, DK_B, DV_B), 0.5),
        'cache_sb_k': nrm((N_ODD, DEC_BATCH, PAST_LEN, H_C, HEAD_DIM), 1.0),
        'cache_sb_v': nrm((N_ODD, DEC_BATCH, PAST_LEN, H_C, HEAD_DIM), 1.0),
        'cache_mem_k': nrm((DEPTH, DEC_BATCH, N_MEM, H_X, HD_X), 1.0),
        'cache_mem_v': nrm((DEPTH, DEC_BATCH, N_MEM, H_X, HD_X), 1.0),
        'ln_g': 1.0 + nrm((DEPTH, 4, D_MODEL), 0.01),
        'ln_b': nrm((DEPTH, 4, D_MODEL), 0.01),
        'ffn1_w_gate': nrm((DEPTH, D_MODEL, D_FF), d_in),
        'ffn1_w_up': nrm((DEPTH, D_MODEL, D_FF), d_in),
        'ffn1_w_down': nrm((DEPTH, D_FF, D_MODEL), BETA * float(D_FF) ** -0.5),
        'ffn2_w_gate': nrm((DEPTH, D_MODEL, D_FF), d_in),
        'ffn2_w_up': nrm((DEPTH, D_MODEL, D_FF), d_in),
        'ffn2_w_down': nrm((DEPTH, D_FF, D_MODEL), BETA * float(D_FF) ** -0.5),
        'x_w_q': nrm((DEPTH, D_MODEL, D_MODEL), d_in),
        'x_w_kv': nrm((DEPTH, D_MODEL, 2 * D_MODEL), d_in),
        'x_w_o': nrm((DEPTH, D_MODEL, D_MODEL), BETA * d_in),
        'ev_w_in': nrm((N_EVEN, D_MODEL, D_IN_EVEN), d_in),
        'fox_b_f': 2.0 + nrm((N_EVEN, H_A), 0.5),
        'hgrn_lb_logits': nrm((N_EVEN + 1, W_B), 0.1),
        'hgrn_norm_g': 1.0 + nrm((N_EVEN, W_B), 0.01),
        'ev_w_out': nrm((N_EVEN, W_A + W_B, D_MODEL), BETA * float(W_A + W_B) ** -0.5),
        'od_w_in': nrm((N_ODD, D_MODEL, D_IN_ODD), d_in),
        'od_w_out': nrm((N_ODD, W_C, D_MODEL), BETA * float(W_C) ** -0.5),
    }


def reference(x_prompt, x_sample, mem_prompt, cache_fox_k, cache_fox_v, cache_fox_logf, state_hgrn,
              cache_sb_k, cache_sb_v, cache_mem_k, cache_mem_v, ln_g, ln_b,
              ffn1_w_gate, ffn1_w_up, ffn1_w_down, ffn2_w_gate, ffn2_w_up, ffn2_w_down,
              x_w_q, x_w_kv, x_w_o, ev_w_in, fox_b_f, hgrn_lb_logits, hgrn_norm_g, ev_w_out,
              od_w_in, od_w_out):
    xp, xs = x_prompt, x_sample
    fk_p, fv_p, flf_p, hs_p, sk_p, sv_p, mk_p, mv_p = [], [], [], [], [], [], [], []
    fk_s, fv_s, flf_s, hs_s, sk_s, sv_s = [], [], [], [], [], []
    for l in range(DEPTH):
        xp = post_norm(xp, swiglu_half(xp, ffn1_w_gate[l], ffn1_w_up[l], ffn1_w_down[l]), ln_g[l, 0], ln_b[l, 0])
        xs = post_norm(xs, swiglu_half(xs, ffn1_w_gate[l], ffn1_w_up[l], ffn1_w_down[l]), ln_g[l, 0], ln_b[l, 0])
        j = l // 2
        if l % 2 == 0:
            lb = hgrn_lower_bound(hgrn_lb_logits, j)
            mp, kp, vp, lfp, sp = even_mixer_prompt(xp, ev_w_in[j], fox_b_f[j], lb, hgrn_norm_g[j], ev_w_out[j])
            ms, kss, vss, lfs, ss = even_mixer_sample(xs, cache_fox_k[j], cache_fox_v[j], cache_fox_logf[j],
                                                      state_hgrn[j], ev_w_in[j], fox_b_f[j], lb,
                                                      hgrn_norm_g[j], ev_w_out[j])
            fk_p.append(kp); fv_p.append(vp); flf_p.append(lfp); hs_p.append(sp)
            fk_s.append(kss); fv_s.append(vss); flf_s.append(lfs); hs_s.append(ss)
        else:
            mp, kp, vp = odd_mixer_prompt(xp, od_w_in[j], od_w_out[j])
            ms, kss, vss = odd_mixer_sample(xs, cache_sb_k[j], cache_sb_v[j], od_w_in[j], od_w_out[j])
            sk_p.append(kp); sv_p.append(vp)
            sk_s.append(kss); sv_s.append(vss)
        xp = post_norm(xp, mp, ln_g[l, 1], ln_b[l, 1])
        xs = post_norm(xs, ms, ln_g[l, 1], ln_b[l, 1])
        mkp, mvp = mem_kv(mem_prompt, x_w_kv[l])
        mk_p.append(mkp); mv_p.append(mvp)
        xp = post_norm(xp, cross_attend(xp, mkp, mvp, x_w_q[l], x_w_o[l]), ln_g[l, 2], ln_b[l, 2])
        xs = post_norm(xs, cross_attend(xs, cache_mem_k[l], cache_mem_v[l], x_w_q[l], x_w_o[l]), ln_g[l, 2], ln_b[l, 2])
        xp = post_norm(xp, swiglu_half(xp, ffn2_w_gate[l], ffn2_w_up[l], ffn2_w_down[l]), ln_g[l, 3], ln_b[l, 3])
        xs = post_norm(xs, swiglu_half(xs, ffn2_w_gate[l], ffn2_w_up[l], ffn2_w_down[l]), ln_g[l, 3], ln_b[l, 3])
    fox_k_prompt = jnp.stack(fk_p)
    fox_v_prompt = jnp.stack(fv_p)
    fox_logf_prompt = jnp.stack(flf_p)
    hgrn_state_prompt = jnp.stack(hs_p)
    sb_k_prompt = jnp.stack(sk_p)
    sb_v_prompt = jnp.stack(sv_p)
    mem_k_prompt = jnp.stack(mk_p)
    mem_v_prompt = jnp.stack(mv_p)
    fox_k_sample = jnp.stack(fk_s)
    fox_v_sample = jnp.stack(fv_s)
    fox_logf_sample = jnp.stack(flf_s)
    hgrn_state_sample = jnp.stack(hs_s)
    sb_k_sample = jnp.stack(sk_s)
    sb_v_sample = jnp.stack(sv_s)
    return (xp, xs, fox_k_prompt, fox_v_prompt, fox_logf_prompt, hgrn_state_prompt, sb_k_prompt, sb_v_prompt,
            mem_k_prompt, mem_v_prompt, fox_k_sample, fox_v_sample, fox_logf_sample, hgrn_state_sample,
            sb_k_sample, sb_v_sample)
```

```python
import functools

import jax
import jax.numpy as jnp
from jax import lax
from jax.experimental import pallas as pl
from jax.experimental.pallas import tpu as pltpu

F32 = jnp.float32
BF16 = jnp.bfloat16

D_MODEL = 2048
DEPTH = 2
HEAD_DIM = 128
H_A = D_MODEL // (2 * HEAD_DIM)
W_A = H_A * HEAD_DIM
H_B = D_MODEL // (2 * HEAD_DIM)
W_B = H_B * 128
H_C = D_MODEL // HEAD_DIM
W_C = H_C * HEAD_DIM
H_X = 4
HD_X = D_MODEL // H_X
N_MEM = 256
D_FF = ((8 * D_MODEL // 3 + 127) // 128) * 128
ALPHA = (2.0 * DEPTH) ** 0.25
LN_EPS = 1e-5
RMS_EPS = 1e-6

LANES = 128
FFN_TILE_F = 512
D_FF_PAD = ((D_FF + FFN_TILE_F - 1) // FFN_TILE_F) * FFN_TILE_F
VMEM_LIMIT_BYTES = 56 * 1024 * 1024
HGRN_BLOCK = 16
NEG_BIG = -1e30

_NT = (((1,), (1,)), ((), ()))
_TN = (((0,), (0,)), ((), ()))


def _params(sem):
    return pltpu.CompilerParams(dimension_semantics=sem, vmem_limit_bytes=VMEM_LIMIT_BYTES)


def _layer_norm(z, g, b):
    mu = jnp.mean(z, axis=-1, keepdims=True)
    zc = z - mu
    var = jnp.mean(zc * zc, axis=-1, keepdims=True)
    return zc * lax.rsqrt(var + LN_EPS) * g + b


def _sigmoid(x):
    return 1.0 / (1.0 + jnp.exp(-x))


def _log_sigmoid(x):
    return jnp.minimum(x, 0.0) - jnp.log1p(jnp.exp(-jnp.abs(x)))


def _ffn_kernel(x_ref, wg_ref, wu_ref, wd_ref, g_ref, b_ref, y_ref, xb_ref):
    j = pl.program_id(1)

    @pl.when(j == 0)
    def _():
        xb_ref[...] = x_ref[...].astype(BF16)
        y_ref[...] = jnp.zeros_like(y_ref)

    xb = xb_ref[...]
    gate = jnp.dot(xb, wg_ref[...], preferred_element_type=F32)
    up = jnp.dot(xb, wu_ref[...], preferred_element_type=F32)
    h = (gate * _sigmoid(gate) * up).astype(BF16)
    y_ref[...] += jnp.dot(h, wd_ref[...], preferred_element_type=F32)

    @pl.when(j == pl.num_programs(1) - 1)
    def _():
        z = ALPHA * x_ref[...] + 0.5 * y_ref[...]
        y_ref[...] = _layer_norm(z, g_ref[...], b_ref[...])


def _ffn(x, wg, wu, wd, g, b, *, tm):
    m = x.shape[0]
    nf = D_FF_PAD // FFN_TILE_F
    return pl.pallas_call(
        _ffn_kernel,
        grid=(m // tm, nf),
        in_specs=[
            pl.BlockSpec((tm, D_MODEL), lambda i, j: (i, 0)),
            pl.BlockSpec((D_MODEL, FFN_TILE_F), lambda i, j: (0, j)),
            pl.BlockSpec((D_MODEL, FFN_TILE_F), lambda i, j: (0, j)),
            pl.BlockSpec((FFN_TILE_F, D_MODEL), lambda i, j: (j, 0)),
            pl.BlockSpec((1, D_MODEL), lambda i, j: (0, 0)),
            pl.BlockSpec((1, D_MODEL), lambda i, j: (0, 0)),
        ],
        out_specs=pl.BlockSpec((tm, D_MODEL), lambda i, j: (i, 0)),
        out_shape=jax.ShapeDtypeStruct((m, D_MODEL), F32),
        scratch_shapes=[pltpu.VMEM((tm, D_MODEL), BF16)],
        compiler_params=_params(("arbitrary", "arbitrary")),
        name="ffn_ln",
    )(x, wg, wu, wd, g, b)


def _mm_kernel(x_ref, w_ref, o_ref, xb_ref):
    @pl.when(pl.program_id(1) == 0)
    def _():
        xb_ref[...] = x_ref[...].astype(BF16)

    o_ref[...] = jnp.dot(xb_ref[...], w_ref[...], preferred_element_type=F32).astype(o_ref.dtype)


def _mm(x, w, *, tm, tn, out_dtype, name):
    m, k = x.shape
    n = w.shape[1]
    return pl.pallas_call(
        _mm_kernel,
        grid=(m // tm, n // tn),
        in_specs=[
            pl.BlockSpec((tm, k), lambda i, j: (i, 0)),
            pl.BlockSpec((k, tn), lambda i, j: (0, j)),
        ],
        out_specs=pl.BlockSpec((tm, tn), lambda i, j: (i, j)),
        out_shape=jax.ShapeDtypeStruct((m, n), out_dtype),
        scratch_shapes=[pltpu.VMEM((tm, k), BF16)],
        compiler_params=_params(("arbitrary", "arbitrary")),
        name=name,
    )(x, w)


def _logf_kernel(x_ref, w_ref, bf_ref, o_ref):
    fa = jnp.dot(x_ref[...].astype(BF16), w_ref[...], preferred_element_type=F32)
    lane = lax.broadcasted_iota(jnp.int32, fa.shape, 1)
    o_ref[...] = jnp.where(lane < H_A, _log_sigmoid(fa + bf_ref[...]), 0.0)


def _logf(x, w_pad, bf_pad, *, tm):
    m = x.shape[0]
    return pl.pallas_call(
        _logf_kernel,
        grid=(m // tm,),
        in_specs=[
            pl.BlockSpec((tm, D_MODEL), lambda i: (i, 0)),
            pl.BlockSpec((D_MODEL, LANES), lambda i: (0, 0)),
            pl.BlockSpec((1, LANES), lambda i: (0, 0)),
        ],
        out_specs=pl.BlockSpec((tm, LANES), lambda i: (i, 0)),
        out_shape=jax.ShapeDtypeStruct((m, LANES), F32),
        compiler_params=_params(("arbitrary",)),
        name="fox_logf",
    )(x, w_pad, bf_pad)


def _lin_ln_kernel(*refs, n_in):
    a_refs = refs[:n_in]
    w_refs = refs[n_in:2 * n_in]
    x_ref, g_ref, b_ref, y_ref = refs[2 * n_in:]
    acc = None
    for a_ref, w_ref in zip(a_refs, w_refs):
        part = jnp.dot(a_ref[...], w_ref[...], preferred_element_type=F32)
        acc = part if acc is None else acc + part
    z = ALPHA * x_ref[...] + acc
    y_ref[...] = _layer_norm(z, g_ref[...], b_ref[...])


def _lin_ln(a_list, w_list, x, g, b, *, tm, name):
    m = x.shape[0]
    n_in = len(a_list)
    in_specs = [pl.BlockSpec((tm, a.shape[1]), lambda i: (i, 0)) for a in a_list]
    in_specs += [pl.BlockSpec(w.shape, lambda i: (0, 0)) for w in w_list]
    in_specs += [
        pl.BlockSpec((tm, D_MODEL), lambda i: (i, 0)),
        pl.BlockSpec((1, D_MODEL), lambda i: (0, 0)),
        pl.BlockSpec((1, D_MODEL), lambda i: (0, 0)),
    ]
    return pl.pallas_call(
        functools.partial(_lin_ln_kernel, n_in=n_in),
        grid=(m // tm,),
        in_specs=in_specs,
        out_specs=pl.BlockSpec((tm, D_MODEL), lambda i: (i, 0)),
        out_shape=jax.ShapeDtypeStruct((m, D_MODEL), F32),
        compiler_params=_params(("arbitrary",)),
        name=name,
    )(*a_list, *w_list, x, g, b)


def _cum_kernel(lf_ref, cc_ref, cr_ref, *, n_blocks):
    r_i = lax.broadcasted_iota(jnp.int32, (LANES, LANES), 0)
    c_i = lax.broadcasted_iota(jnp.int32, (LANES, LANES), 1)
    lower = (r_i >= c_i).astype(F32)
    upper = (r_i <= c_i).astype(F32)
    e_r = lax.broadcasted_iota(jnp.int32, (8, LANES), 0)
    e_c = lax.broadcasted_iota(jnp.int32, (8, LANES), 1)
    pick = (e_r == e_c).astype(F32)

    def body(r, carry):
        carry_row, carry_col = carry
        off = pl.multiple_of(r * LANES, LANES)
        lf = lf_ref[pl.ds(off, LANES), :]
        cc = jnp.dot(lower, lf, preferred_element_type=F32, precision=lax.Precision.HIGHEST) + carry_row
        cc_ref[pl.ds(off, LANES), :] = cc
        lf_row = lax.dot_general(pick, lf, _NT, preferred_element_type=F32, precision=lax.Precision.HIGHEST)
        cr = jnp.dot(lf_row, upper, preferred_element_type=F32, precision=lax.Precision.HIGHEST) + carry_col
        cr_ref[:, pl.ds(off, LANES)] = cr
        return cc[LANES - 1:LANES, :], cr[:, LANES - 1:LANES]

    lax.fori_loop(0, n_blocks, body, (jnp.zeros((1, LANES), F32), jnp.zeros((8, 1), F32)))


def _cum(lf_pad, *, nb, skv):
    return pl.pallas_call(
        functools.partial(_cum_kernel, n_blocks=skv // LANES),
        grid=(nb,),
        in_specs=[pl.BlockSpec((skv, LANES), lambda b: (b, 0))],
        out_specs=[
            pl.BlockSpec((skv, LANES), lambda b: (b, 0)),
            pl.BlockSpec((None, 8, skv), lambda b: (b, 0, 0)),
        ],
        out_shape=[
            jax.ShapeDtypeStruct((nb * skv, LANES), F32),
            jax.ShapeDtypeStruct((nb, 8, skv), F32),
        ],
        compiler_params=_params(("arbitrary",)),
        name="fox_cum",
    )(lf_pad)


def _fox_kernel(q_ref, k_ref, v_ref, cq_ref, ck_ref, o_ref, kb_ref, vb_ref, *, q_pos0, tq, tk):
    h = pl.program_id(1)
    i = pl.program_id(2)

    @pl.when(i == 0)
    def _():
        kb_ref[...] = k_ref[...].astype(BF16)
        vb_ref[...] = v_ref[...].astype(BF16)

    q = q_ref[...]
    lane = lax.broadcasted_iota(jnp.int32, (tq, LANES), 1)
    cq = jnp.sum(jnp.where(lane == h, cq_ref[...], 0.0), axis=-1, keepdims=True)
    qpos0 = q_pos0 + i * tq
    n_full = (qpos0 + 1) // tk
    n_all = (qpos0 + tq - 1) // tk + 1
    scale = HEAD_DIM ** -0.5

    def step(j, carry, masked):
        m, l, acc = carry
        off = pl.multiple_of(j * tk, tk)
        kj = kb_ref[pl.ds(off, tk), :]
        vj = vb_ref[pl.ds(off, tk), :]
        s = lax.dot_general(q, kj, _NT, preferred_element_type=F32) * scale
        s = s + (cq - ck_ref[:, pl.ds(off, tk)])
        if masked:
            qp = qpos0 + lax.broadcasted_iota(jnp.int32, (tq, tk), 0)
            kp = off + lax.broadcasted_iota(jnp.int32, (tq, tk), 1)
            s = jnp.where(qp >= kp, s, -jnp.inf)
        m_new = jnp.maximum(m, jnp.max(s, axis=-1, keepdims=True))
        a = jnp.exp(m - m_new)
        p = jnp.exp(s - m_new)
        l = a * l + jnp.sum(p, axis=-1, keepdims=True)
        acc = a * acc + jnp.dot(p.astype(BF16), vj, preferred_element_type=F32)
        return m_new, l, acc

    init = (jnp.full((tq, 1), NEG_BIG, F32), jnp.zeros((tq, 1), F32), jnp.zeros((tq, HEAD_DIM), F32))
    carry = lax.fori_loop(0, n_full, functools.partial(step, masked=False), init)
    _, l, acc = lax.fori_loop(n_full, n_all, functools.partial(step, masked=True), carry)
    o_ref[...] = (acc / l).astype(o_ref.dtype)


def _fox(q, k, v, cum_col, cum_row, *, nb, sq, skv, q_pos0, tq, tk):
    nq = sq // tq
    cq_blk0 = q_pos0 // tq
    cq_per_b = skv // tq
    return pl.pallas_call(
        functools.partial(_fox_kernel, q_pos0=q_pos0, tq=tq, tk=tk),
        grid=(nb, H_A, nq),
        in_specs=[
            pl.BlockSpec((tq, HEAD_DIM), lambda b, h, i: (b * nq + i, h)),
            pl.BlockSpec((skv, HEAD_DIM), lambda b, h, i: (b, h)),
            pl.BlockSpec((skv, HEAD_DIM), lambda b, h, i: (b, h)),
            pl.BlockSpec((tq, LANES), lambda b, h, i: (b * cq_per_b + cq_blk0 + i, 0)),
            pl.BlockSpec((None, None, 1, skv), lambda b, h, i: (b, h, 0, 0)),
        ],
        out_specs=pl.BlockSpec((tq, HEAD_DIM), lambda b, h, i: (b * nq + i, h)),
        out_shape=jax.ShapeDtypeStruct((nb * sq, W_A), BF16),
        scratch_shapes=[pltpu.VMEM((skv, HEAD_DIM), BF16), pltpu.VMEM((skv, HEAD_DIM), BF16)],
        compiler_params=_params(("arbitrary", "arbitrary", "arbitrary")),
        name="fox_attn",
    )(q, k, v, cum_col, cum_row)


def _sb_kernel(q_ref, k_ref, v_ref, u_ref, o_ref, kb_ref, vb_ref, *, q_pos0, tq, tk):
    i = pl.program_id(2)

    @pl.when(i == 0)
    def _():
        kb_ref[...] = k_ref[...].astype(BF16)
        vb_ref[...] = v_ref[...].astype(BF16)

    q = q_ref[...]
    u = u_ref[...]
    qpos0 = q_pos0 + i * tq
    n_full = qpos0 // tk
    n_all = (qpos0 + tq - 2) // tk + 1
    scale = HEAD_DIM ** -0.5

    def step(r, carry, masked, j_top):
        rem_run, acc = carry
        j = j_top - r
        off = pl.multiple_of(j * tk, tk)
        kj = kb_ref[pl.ds(off, tk), :]
        vj = vb_ref[pl.ds(off, tk), :]
        z = lax.dot_general(q, kj, _NT, preferred_element_type=F32) * scale
        ls = _log_sigmoid(z)
        l1m = ls - z
        if masked:
            qp = qpos0 + lax.broadcasted_iota(jnp.int32, (tq, tk), 0)
            kp = off + lax.broadcasted_iota(jnp.int32, (tq, tk), 1)
            mask = kp < qp
            l1m = jnp.where(mask, l1m, 0.0)
        hi = l1m.astype(BF16)
        lo = (l1m - hi.astype(F32)).astype(BF16)
        rem = jnp.dot(hi, u, preferred_element_type=F32) + jnp.dot(lo, u, preferred_element_type=F32) + rem_run
        w = jnp.exp(ls + rem)
        if masked:
            w = jnp.where(mask, w, 0.0)
        acc = acc + jnp.dot(w.astype(BF16), vj, preferred_element_type=F32)
        rem_run = rem_run + jnp.sum(l1m, axis=-1, keepdims=True)
        return rem_run, acc

    init = (jnp.zeros((tq, 1), F32), jnp.zeros((tq, HEAD_DIM), F32))
    carry = lax.fori_loop(0, n_all - n_full, functools.partial(step, masked=True, j_top=n_all - 1), init)
    _, acc = lax.fori_loop(0, n_full, functools.partial(step, masked=False, j_top=n_full - 1), carry)
    o_ref[...] = acc.astype(o_ref.dtype)


def _sb(q, k, v, u, *, nb, sq, skv, q_pos0, tq, tk):
    nq = sq // tq
    return pl.pallas_call(
        functools.partial(_sb_kernel, q_pos0=q_pos0, tq=tq, tk=tk),
        grid=(nb, H_C, nq),
        in_specs=[
            pl.BlockSpec((tq, HEAD_DIM), lambda b, h, i: (b * nq + i, h)),
            pl.BlockSpec((skv, HEAD_DIM), lambda b, h, i: (b, h)),
            pl.BlockSpec((skv, HEAD_DIM), lambda b, h, i: (b, h)),
            pl.BlockSpec((tk, tk), lambda b, h, i: (0, 0)),
        ],
        out_specs=pl.BlockSpec((tq, HEAD_DIM), lambda b, h, i: (b * nq + i, h)),
        out_shape=jax.ShapeDtypeStruct((nb * sq, W_C), BF16),
        scratch_shapes=[pltpu.VMEM((skv, HEAD_DIM), BF16), pltpu.VMEM((skv, HEAD_DIM), BF16)],
        compiler_params=_params(("arbitrary", "arbitrary", "arbitrary")),
        name="sb_attn",
    )(q, k, v, u)


def _cross_kernel(q_ref, k_ref, v_ref, o_ref):
    q = q_ref[...]
    k = k_ref[...].astype(BF16)
    v = v_ref[...].astype(BF16)
    s = lax.dot_general(q, k, _NT, preferred_element_type=F32) * (HD_X ** -0.5)
    m = jnp.max(s, axis=-1, keepdims=True)
    p = jnp.exp(s - m)
    l = jnp.sum(p, axis=-1, keepdims=True)
    o = jnp.dot(p.astype(BF16), v, preferred_element_type=F32) / l
    o_ref[...] = o.astype(o_ref.dtype)


def _cross(q, mk, mv, *, nb, sq, tq):
    nq = sq // tq
    return pl.pallas_call(
        _cross_kernel,
        grid=(nb, H_X, nq),
        in_specs=[
            pl.BlockSpec((tq, HD_X), lambda b, h, i: (b * nq + i, h)),
            pl.BlockSpec((N_MEM, HD_X), lambda b, h, i: (b, h)),
            pl.BlockSpec((N_MEM, HD_X), lambda b, h, i: (b, h)),
        ],
        out_specs=pl.BlockSpec((tq, HD_X), lambda b, h, i: (b * nq + i, h)),
        out_shape=jax.ShapeDtypeStruct((nb * sq, D_MODEL), BF16),
        compiler_params=_params(("arbitrary", "arbitrary", "arbitrary")),
        name="cross_attn",
    )(q, mk, mv)


def _hgrn_kernel(q_ref, f_ref, i_ref, g_ref, lb_ref, ng_ref, s0_ref, o_ref, sf_ref, st_ref, *, heads, ts):
    t = pl.program_id(2)
    lbk = HGRN_BLOCK

    @pl.when(t == 0)
    def _():
        for gi in range(heads):
            st_ref[gi] = s0_ref[gi].T

    row = lax.broadcasted_iota(jnp.int32, (lbk, LANES), 0)

    def block(r, carry):
        off = pl.multiple_of(r * lbk, lbk)
        for gi in range(heads):
            cs = slice(gi * LANES, (gi + 1) * LANES)
            xq = q_ref[pl.ds(off, lbk), cs]
            xf = f_ref[pl.ds(off, lbk), cs]
            xi = i_ref[pl.ds(off, lbk), cs]
            xg = g_ref[pl.ds(off, lbk), cs]
            lb = lb_ref[:, cs]
            f = lb + (1.0 - lb) * _sigmoid(xf)
            kk = 1.0 - f
            qq = xq * _sigmoid(xq)
            bcum = jnp.log(f)
            sh = 1
            while sh < lbk:
                bcum = bcum + jnp.where(row >= sh, pltpu.roll(bcum, sh, 0), 0.0)
                sh *= 2
            o = jnp.sum(qq * kk, axis=-1, keepdims=True) * xi
            for d in range(1, lbk):
                e = jnp.exp(bcum - pltpu.roll(bcum, d, 0))
                wgt = jnp.sum(jnp.where(row >= d, qq * pltpu.roll(kk, d, 0) * e, 0.0), axis=-1, keepdims=True)
                o = o + wgt * pltpu.roll(xi, d, 0)
            st = st_ref[gi]
            qe = (qq * jnp.exp(bcum)).astype(BF16)
            o = o + lax.dot_general(qe, st.astype(BF16), _NT, preferred_element_type=F32)
            b_last = bcum[lbk - 1:lbk, :]
            ke = (kk * jnp.exp(b_last - bcum)).astype(BF16)
            st_ref[gi] = st * jnp.exp(b_last) + lax.dot_general(xi.astype(BF16), ke, _TN, preferred_element_type=F32)
            rms = lax.rsqrt(jnp.mean(o * o, axis=-1, keepdims=True) + RMS_EPS)
            gated = o * rms * ng_ref[:, cs] * (xg * _sigmoid(xg))
            o_ref[pl.ds(off, lbk), cs] = gated.astype(o_ref.dtype)
        return carry

    lax.fori_loop(0, ts // lbk, block, 0)

    @pl.when(t == pl.num_programs(2) - 1)
    def _():
        for gi in range(heads):
            sf_ref[gi] = st_ref[gi].T


def _hgrn(hb, lb, ng, s0, *, nb, s, ts, heads):
    nt = s // ts
    ng_groups = H_B // heads
    wblk = heads * LANES

    def col(seg):
        return lambda b, hg, t: (b * nt + t, seg * ng_groups + hg)

    return pl.pallas_call(
        functools.partial(_hgrn_kernel, heads=heads, ts=ts),
        grid=(nb, ng_groups, nt),
        in_specs=[
            pl.BlockSpec((ts, wblk), col(0)),
            pl.BlockSpec((ts, wblk), col(1)),
            pl.BlockSpec((ts, wblk), col(2)),
            pl.BlockSpec((ts, wblk), col(3)),
            pl.BlockSpec((1, wblk), lambda b, hg, t: (0, hg)),
            pl.BlockSpec((1, wblk), lambda b, hg, t: (0, hg)),
            pl.BlockSpec((None, heads, LANES, LANES), lambda b, hg, t: (b, hg, 0, 0)),
        ],
        out_specs=[
            pl.BlockSpec((ts, wblk), lambda b, hg, t: (b * nt + t, hg)),
            pl.BlockSpec((None, heads, LANES, LANES), lambda b, hg, t: (b, hg, 0, 0)),
        ],
        out_shape=[
            jax.ShapeDtypeStruct((nb * s, W_B), BF16),
            jax.ShapeDtypeStruct((nb, H_B, LANES, LANES), F32),
        ],
        scratch_shapes=[pltpu.VMEM((heads, LANES, LANES), F32)],
        compiler_params=_params(("arbitrary", "arbitrary", "arbitrary")),
        name="hgrn",
    )(hb, hb, hb, hb, lb, ng, s0)


def _bf16(w):
    return w.astype(BF16)


def _pad_rows(a, n):
    return jnp.pad(a, ((0, n - a.shape[0]), (0, 0)))


def _cat_cache(cache, new, nb, t_new, s_pad):
    p = cache.shape[1]
    width = new.shape[1]
    parts = [cache.reshape(nb, p, width), new.reshape(nb, t_new, width),
             jnp.zeros((nb, s_pad - p - t_new, width), new.dtype)]
    return jnp.concatenate(parts, axis=1).reshape(nb * s_pad, width)


def kernel(x_prompt, x_sample, mem_prompt, cache_fox_k, cache_fox_v, cache_fox_logf, state_hgrn, cache_sb_k, cache_sb_v, cache_mem_k, cache_mem_v, ln_g, ln_b, ffn1_w_gate, ffn1_w_up, ffn1_w_down, ffn2_w_gate, ffn2_w_up, ffn2_w_down, x_w_q, x_w_kv, x_w_o, ev_w_in, fox_b_f, hgrn_lb_logits, hgrn_norm_g, ev_w_out, od_w_in, od_w_out):
    nbp, sp, _ = x_prompt.shape
    nbs, ss, _ = x_sample.shape
    past = cache_fox_k.shape[2]
    s_all = ((past + ss + LANES - 1) // LANES) * LANES
    tm_p, tm_s = 512, nbs * ss
    xp = x_prompt.reshape(nbp * sp, D_MODEL)
    xs = x_sample.reshape(nbs * ss, D_MODEL)
    mem = mem_prompt.reshape(nbp * N_MEM, D_MODEL)

    def ffn_weights(wg, wu, wd):
        padc = ((0, 0), (0, D_FF_PAD - D_FF))
        return (_bf16(jnp.pad(wg, padc)), _bf16(jnp.pad(wu, padc)), _bf16(_pad_rows(wd, D_FF_PAD)))

    def sb_u(tk):
        r = lax.broadcasted_iota(jnp.int32, (tk, tk), 0)
        c = lax.broadcasted_iota(jnp.int32, (tk, tk), 1)
        return (r > c).astype(BF16)

    outs = {k: [] for k in ("fk_p", "fv_p", "flf_p", "hs_p", "sk_p", "sv_p", "mk_p", "mv_p",
                            "fk_s", "fv_s", "flf_s", "hs_s", "sk_s", "sv_s")}
    for l in range(DEPTH):
        g = [ln_g[l, k].reshape(1, D_MODEL) for k in range(4)]
        bb = [ln_b[l, k].reshape(1, D_MODEL) for k in range(4)]
        j = l // 2

        w1 = ffn_weights(ffn1_w_gate[l], ffn1_w_up[l], ffn1_w_down[l])
        xp = _ffn(xp, *w1, g[0], bb[0], tm=tm_p)
        xs = _ffn(xs, *w1, g[0], bb[0], tm=tm_s)

        if l % 2 == 0:
            w_in = ev_w_in[j]
            wq, wk, wv = (_bf16(w_in[:, k * W_A:(k + 1) * W_A]) for k in range(3))
            wf = _bf16(jnp.pad(w_in[:, 3 * W_A:3 * W_A + H_A], ((0, 0), (0, LANES - H_A))))
            wb = _bf16(w_in[:, 3 * W_A + H_A:])
            bf_pad = jnp.pad(fox_b_f[j].astype(F32), (0, LANES - H_A)).reshape(1, LANES)
            lb = jnp.cumsum(jax.nn.softmax(hgrn_lb_logits.astype(F32), axis=0), axis=0)[j].reshape(1, W_B)
            ng = hgrn_norm_g[j].astype(F32).reshape(1, W_B)
            w_out = ev_w_out[j]
            wo_a, wo_b = _bf16(w_out[:W_A]), _bf16(w_out[W_A:])

            qa = _mm(xp, wq, tm=tm_p, tn=512, out_dtype=BF16, name="ev_q")
            ka = _mm(xp, wk, tm=tm_p, tn=512, out_dtype=F32, name="ev_k")
            va = _mm(xp, wv, tm=tm_p, tn=512, out_dtype=F32, name="ev_v")
            hb = _mm(xp, wb, tm=tm_p, tn=512, out_dtype=F32, name="ev_b")
            lf = _logf(xp, wf, bf_pad, tm=tm_p)
            cum_col, cum_row = _cum(lf, nb=nbp, skv=sp)
            oa = _fox(qa, ka, va, cum_col, cum_row.reshape(nbp, 8, 1, sp), nb=nbp, sq=sp, skv=sp,
                      q_pos0=0, tq=256, tk=256)
            ob, s_fin = _hgrn(hb, lb, ng, jnp.zeros((nbp, H_B, LANES, LANES), F32), nb=nbp, s=sp, ts=512, heads=4)
            outs["fk_p"].append(ka.reshape(nbp, sp, H_A, HEAD_DIM))
            outs["fv_p"].append(va.reshape(nbp, sp, H_A, HEAD_DIM))
            outs["flf_p"].append(lf[:, :H_A].reshape(nbp, sp, H_A))
            outs["hs_p"].append(s_fin)
            xp = _lin_ln([oa, ob], [wo_a, wo_b], xp, g[1], bb[1], tm=tm_p, name="ev_out_ln")

            qa = _mm(xs, wq, tm=tm_s, tn=512, out_dtype=BF16, name="ev_q_s")
            ka = _mm(xs, wk, tm=tm_s, tn=512, out_dtype=F32, name="ev_k_s")
            va = _mm(xs, wv, tm=tm_s, tn=512, out_dtype=F32, name="ev_v_s")
            hb = _mm(xs, wb, tm=tm_s, tn=512, out_dtype=F32, name="ev_b_s")
            lf = _logf(xs, wf, bf_pad, tm=tm_s)
            c_lf = jnp.pad(cache_fox_logf[j].astype(F32), ((0, 0), (0, 0), (0, LANES - H_A)))
            lf_all = jnp.concatenate(
                [c_lf, lf.reshape(nbs, ss, LANES), jnp.zeros((nbs, s_all - past - ss, LANES), F32)], axis=1)
            cum_col, cum_row = _cum(lf_all.reshape(nbs * s_all, LANES), nb=nbs, skv=s_all)
            k_all = _cat_cache(cache_fox_k[j], ka, nbs, ss, s_all)
            v_all = _cat_cache(cache_fox_v[j], va, nbs, ss, s_all)
            oa = _fox(qa, k_all, v_all, cum_col, cum_row.reshape(nbs, 8, 1, s_all), nb=nbs, sq=ss, skv=s_all,
                      q_pos0=past, tq=ss, tk=LANES)
            ob, s_new = _hgrn(hb, lb, ng, state_hgrn[j].astype(F32), nb=nbs, s=ss, ts=ss, heads=4)
            outs["fk_s"].append(ka.reshape(nbs, ss, H_A, HEAD_DIM))
            outs["fv_s"].append(va.reshape(nbs, ss, H_A, HEAD_DIM))
            outs["flf_s"].append(lf[:, :H_A].reshape(nbs, ss, H_A))
            outs["hs_s"].append(s_new)
            xs = _lin_ln([oa, ob], [wo_a, wo_b], xs, g[1], bb[1], tm=tm_s, name="ev_out_ln_s")
        else:
            w_in = od_w_in[j]
            wq, wk, wv = (_bf16(w_in[:, k * W_C:(k + 1) * W_C]) for k in range(3))
            wo = _bf16(od_w_out[j])

            q = _mm(xp, wq, tm=tm_p, tn=512, out_dtype=BF16, name="od_q")
            k = _mm(xp, wk, tm=tm_p, tn=512, out_dtype=F32, name="od_k")
            v = _mm(xp, wv, tm=tm_p, tn=512, out_dtype=F32, name="od_v")
            o = _sb(q, k, v, sb_u(256), nb=nbp, sq=sp, skv=sp, q_pos0=0, tq=256, tk=256)
            outs["sk_p"].append(k.reshape(nbp, sp, H_C, HEAD_DIM))
            outs["sv_p"].append(v.reshape(nbp, sp, H_C, HEAD_DIM))
            xp = _lin_ln([o], [wo], xp, g[1], bb[1], tm=tm_p, name="od_out_ln")

            q = _mm(xs, wq, tm=tm_s, tn=512, out_dtype=BF16, name="od_q_s")
            k = _mm(xs, wk, tm=tm_s, tn=512, out_dtype=F32, name="od_k_s")
            v = _mm(xs, wv, tm=tm_s, tn=512, out_dtype=F32, name="od_v_s")
            k_all = _cat_cache(cache_sb_k[j], k, nbs, ss, s_all)
            v_all = _cat_cache(cache_sb_v[j], v, nbs, ss, s_all)
            o = _sb(q, k_all, v_all, sb_u(LANES), nb=nbs, sq=ss, skv=s_all, q_pos0=past, tq=ss, tk=LANES)
            outs["sk_s"].append(k.reshape(nbs, ss, H_C, HEAD_DIM))
            outs["sv_s"].append(v.reshape(nbs, ss, H_C, HEAD_DIM))
            xs = _lin_ln([o], [wo], xs, g[1], bb[1], tm=tm_s, name="od_out_ln_s")

        wxq = _bf16(x_w_q[l])
        wkv = _bf16(x_w_kv[l])
        wxo = _bf16(x_w_o[l])
        mk = _mm(mem, wkv[:, :D_MODEL], tm=512, tn=512, out_dtype=F32, name="mem_k")
        mv = _mm(mem, wkv[:, D_MODEL:], tm=512, tn=512, out_dtype=F32, name="mem_v")
        outs["mk_p"].append(mk.reshape(nbp, N_MEM, H_X, HD_X))
        outs["mv_p"].append(mv.reshape(nbp, N_MEM, H_X, HD_X))
        qx = _mm(xp, wxq, tm=tm_p, tn=512, out_dtype=BF16, name="x_q")
        ox = _cross(qx, mk, mv, nb=nbp, sq=sp, tq=512)
        xp = _lin_ln([ox], [wxo], xp, g[2], bb[2], tm=tm_p, name="x_out_ln")
        qx = _mm(xs, wxq, tm=tm_s, tn=512, out_dtype=BF16, name="x_q_s")
        ox = _cross(qx, cache_mem_k[l].reshape(nbs * N_MEM, D_MODEL).astype(F32),
                    cache_mem_v[l].reshape(nbs * N_MEM, D_MODEL).astype(F32), nb=nbs, sq=ss, tq=ss)
        xs = _lin_ln([ox], [wxo], xs, g[2], bb[2], tm=tm_s, name="x_out_ln_s")

        w2 = ffn_weights(ffn2_w_gate[l], ffn2_w_up[l], ffn2_w_down[l])
        xp = _ffn(xp, *w2, g[3], bb[3], tm=tm_p)
        xs = _ffn(xs, *w2, g[3], bb[3], tm=tm_s)

    st = {k: jnp.stack(v) for k, v in outs.items()}
    return (xp.reshape(nbp, sp, D_MODEL), xs.reshape(nbs, ss, D_MODEL),
            st["fk_p"], st["fv_p"], st["flf_p"], st["hs_p"], st["sk_p"], st["sv_p"], st["mk_p"], st["mv_p"],
            st["fk_s"], st["fv_s"], st["flf_s"], st["hs_s"], st["sk_s"], st["sv_s"])
```

```python
import functools

import jax
import jax.numpy as jnp
from jax import lax
from jax.experimental import pallas as pl
from jax.experimental.pallas import tpu as pltpu

F32 = jnp.float32
BF16 = jnp.bfloat16

D_MODEL = 2048
DEPTH = 2
HEAD_DIM = 128
H_A = D_MODEL // (2 * HEAD_DIM)
W_A = H_A * HEAD_DIM
H_B = D_MODEL // (2 * HEAD_DIM)
W_B = H_B * 128
H_C = D_MODEL // HEAD_DIM
W_C = H_C * HEAD_DIM
H_X = 4
HD_X = D_MODEL // H_X
N_MEM = 256
D_FF = ((8 * D_MODEL // 3 + 127) // 128) * 128
ALPHA = (2.0 * DEPTH) ** 0.25
LN_EPS = 1e-5
RMS_EPS = 1e-6

LANES = 128
VMEM_LIMIT_BYTES = 56 * 1024 * 1024
FFN_MAIN = 512
FFN_TAIL = 128
FFN_N_MAIN = D_FF // FFN_MAIN
FFN_N_TAIL = (D_FF - FFN_N_MAIN * FFN_MAIN) // FFN_TAIL
FFN_TAIL_BLOCK0 = FFN_N_MAIN * FFN_MAIN // FFN_TAIL
assert FFN_N_MAIN * FFN_MAIN + FFN_N_TAIL * FFN_TAIL == D_FF
TM_PROMPT = 512
ATTN_TQ = 256
ATTN_HEADS = 2
HGRN_BLOCK = 16
HGRN_HEADS = 4
HGRN_TS = 512
NEG_BIG = -1e30

_NT = (((1,), (1,)), ((), ()))
_TN = (((0,), (0,)), ((), ()))


def _params(sem):
    return pltpu.CompilerParams(dimension_semantics=sem, vmem_limit_bytes=VMEM_LIMIT_BYTES)


def _resident(shape, index_map):
    return pl.BlockSpec(shape, index_map, pipeline_mode=pl.Buffered(1))


def _layer_norm(z, g, b):
    mu = jnp.mean(z, axis=-1, keepdims=True)
    zc = z - mu
    var = jnp.mean(zc * zc, axis=-1, keepdims=True)
    return zc * lax.rsqrt(var + LN_EPS) * g + b


def _sigmoid(x):
    return 1.0 / (1.0 + jnp.exp(-x))


def _log_sigmoid(x):
    return jnp.minimum(x, 0.0) - jnp.log(1.0 + jnp.exp(-jnp.abs(x)))


def _ffn_kernel(x_ref, wg_ref, wu_ref, wd_ref, wgt_ref, wut_ref, wdt_ref, g_ref, b_ref, y_ref, xb_ref):
    j = pl.program_id(1)

    @pl.when(j == 0)
    def _():
        xb_ref[...] = x_ref[...].astype(BF16)
        y_ref[...] = jnp.zeros_like(y_ref)

    def accumulate(wg, wu, wd):
        xb = xb_ref[...]
        gate = jnp.dot(xb, wg[...], preferred_element_type=F32)
        up = jnp.dot(xb, wu[...], preferred_element_type=F32)
        h = (gate * _sigmoid(gate) * up).astype(BF16)
        y_ref[...] += jnp.dot(h, wd[...], preferred_element_type=F32)

    @pl.when(j < FFN_N_MAIN)
    def _():
        accumulate(wg_ref, wu_ref, wd_ref)

    @pl.when(j >= FFN_N_MAIN)
    def _():
        accumulate(wgt_ref, wut_ref, wdt_ref)

    @pl.when(j == pl.num_programs(1) - 1)
    def _():
        z = ALPHA * x_ref[...] + 0.5 * y_ref[...]
        y_ref[...] = _layer_norm(z, g_ref[...], b_ref[...])


def _ffn(x, wg, wu, wd, g, b, *, tm):
    m = x.shape[0]

    def main(j):
        return jnp.minimum(j, FFN_N_MAIN - 1)

    def tail(j):
        return FFN_TAIL_BLOCK0 + jnp.maximum(j - FFN_N_MAIN, 0)

    return pl.pallas_call(
        _ffn_kernel,
        grid=(m // tm, FFN_N_MAIN + FFN_N_TAIL),
        in_specs=[
            pl.BlockSpec((tm, D_MODEL), lambda i, j: (i, 0)),
            pl.BlockSpec((D_MODEL, FFN_MAIN), lambda i, j: (0, main(j))),
            pl.BlockSpec((D_MODEL, FFN_MAIN), lambda i, j: (0, main(j))),
            pl.BlockSpec((FFN_MAIN, D_MODEL), lambda i, j: (main(j), 0)),
            pl.BlockSpec((D_MODEL, FFN_TAIL), lambda i, j: (0, tail(j))),
            pl.BlockSpec((D_MODEL, FFN_TAIL), lambda i, j: (0, tail(j))),
            pl.BlockSpec((FFN_TAIL, D_MODEL), lambda i, j: (tail(j), 0)),
            pl.BlockSpec((1, D_MODEL), lambda i, j: (0, 0)),
            pl.BlockSpec((1, D_MODEL), lambda i, j: (0, 0)),
        ],
        out_specs=pl.BlockSpec((tm, D_MODEL), lambda i, j: (i, 0)),
        out_shape=jax.ShapeDtypeStruct((m, D_MODEL), F32),
        scratch_shapes=[pltpu.VMEM((tm, D_MODEL), BF16)],
        compiler_params=_params(("arbitrary", "arbitrary")),
        name="ffn_ln",
    )(x, wg, wu, wd, wg, wu, wd, g, b)


def _proj_kernel(x_ref, w_ref, *refs, kind, n_heads, tm):
    outs, wb_ref = refs[:-1], refs[-1]

    @pl.when(pl.program_id(0) == 0)
    def _():
        wb_ref[...] = w_ref[...].astype(BF16)

    y = jnp.dot(x_ref[...].astype(BF16), wb_ref[...], preferred_element_type=F32)
    if kind == "kv":
        outs[0][...] = y.astype(BF16)
        for h in range(n_heads):
            outs[1][pl.ds(h, tm, stride=n_heads), :] = y[:, h * LANES:(h + 1) * LANES]
    else:
        outs[0][...] = y.astype(outs[0].dtype)


def _proj(x, w, *, col, tn, tm, kind, name):
    m, k = x.shape
    n_heads = tn // LANES
    if kind == "kv":
        out_specs = [pl.BlockSpec((tm, tn), lambda i: (i, 0)),
                     pl.BlockSpec((tm * n_heads, LANES), lambda i: (i, 0))]
        out_shape = [jax.ShapeDtypeStruct((m, tn), BF16), jax.ShapeDtypeStruct((m * n_heads, LANES), F32)]
    else:
        out_specs = pl.BlockSpec((tm, tn), lambda i: (i, 0))
        out_shape = jax.ShapeDtypeStruct((m, tn), BF16 if kind == "bf16" else F32)
    return pl.pallas_call(
        functools.partial(_proj_kernel, kind=kind, n_heads=n_heads, tm=tm),
        grid=(m // tm,),
        in_specs=[
            pl.BlockSpec((tm, k), lambda i: (i, 0)),
            _resident((k, tn), lambda i: (0, col)),
        ],
        out_specs=out_specs,
        out_shape=out_shape,
        scratch_shapes=[pltpu.VMEM((k, tn), BF16)],
        compiler_params=_params(("arbitrary",)),
        name=name,
    )(x, w)


def _logf_kernel(x_ref, w_ref, bf_ref, o_ref):
    fa = jnp.dot(x_ref[...].astype(BF16), w_ref[...], preferred_element_type=F32)
    lane = lax.broadcasted_iota(jnp.int32, fa.shape, 1)
    o_ref[...] = jnp.where(lane < H_A, _log_sigmoid(fa + bf_ref[...]), 0.0)


def _logf(x, w_pad, bf_pad, *, tm):
    m = x.shape[0]
    return pl.pallas_call(
        _logf_kernel,
        grid=(m // tm,),
        in_specs=[
            pl.BlockSpec((tm, D_MODEL), lambda i: (i, 0)),
            pl.BlockSpec((D_MODEL, LANES), lambda i: (0, 0)),
            pl.BlockSpec((1, LANES), lambda i: (0, 0)),
        ],
        out_specs=pl.BlockSpec((tm, LANES), lambda i: (i, 0)),
        out_shape=jax.ShapeDtypeStruct((m, LANES), F32),
        compiler_params=_params(("arbitrary",)),
        name="fox_logf",
    )(x, w_pad, bf_pad)


def _lin_ln_kernel(*refs, n_in):
    a_refs = refs[:n_in]
    w_refs = refs[n_in:2 * n_in]
    x_ref, g_ref, b_ref, y_ref = refs[2 * n_in:2 * n_in + 4]
    wb_refs = refs[2 * n_in + 4:]

    @pl.when(pl.program_id(0) == 0)
    def _():
        for w_ref, wb_ref in zip(w_refs, wb_refs):
            wb_ref[...] = w_ref[...].astype(BF16)

    acc = None
    for a_ref, wb_ref in zip(a_refs, wb_refs):
        part = jnp.dot(a_ref[...], wb_ref[...], preferred_element_type=F32)
        acc = part if acc is None else acc + part
    z = ALPHA * x_ref[...] + acc
    y_ref[...] = _layer_norm(z, g_ref[...], b_ref[...])


def _lin_ln(a_list, w, x, g, b, *, tm, name):
    m = x.shape[0]
    n_in = len(a_list)
    rows = [a.shape[1] for a in a_list]
    assert sum(rows) == w.shape[0] and len(set(rows)) == 1
    in_specs = [pl.BlockSpec((tm, r), lambda i: (i, 0)) for r in rows]
    in_specs += [_resident((r, D_MODEL), functools.partial(lambda i, kk: (kk, 0), kk=kk)) for kk, r in enumerate(rows)]
    in_specs += [
        pl.BlockSpec((tm, D_MODEL), lambda i: (i, 0)),
        pl.BlockSpec((1, D_MODEL), lambda i: (0, 0)),
        pl.BlockSpec((1, D_MODEL), lambda i: (0, 0)),
    ]
    return pl.pallas_call(
        functools.partial(_lin_ln_kernel, n_in=n_in),
        grid=(m // tm,),
        in_specs=in_specs,
        out_specs=pl.BlockSpec((tm, D_MODEL), lambda i: (i, 0)),
        out_shape=jax.ShapeDtypeStruct((m, D_MODEL), F32),
        scratch_shapes=[pltpu.VMEM((r, D_MODEL), BF16) for r in rows],
        compiler_params=_params(("arbitrary",)),
        name=name,
    )(*a_list, *([w] * n_in), x, g, b)


def _cum_kernel(lf_ref, cc_ref, cr_ref, *, n_blocks):
    r_i = lax.broadcasted_iota(jnp.int32, (LANES, LANES), 0)
    c_i = lax.broadcasted_iota(jnp.int32, (LANES, LANES), 1)
    lower = (r_i >= c_i).astype(F32)
    upper = (r_i <= c_i).astype(F32)
    e_r = lax.broadcasted_iota(jnp.int32, (8, LANES), 0)
    e_c = lax.broadcasted_iota(jnp.int32, (8, LANES), 1)
    pick = (e_r == e_c).astype(F32)

    def body(r, carry):
        carry_row, carry_col = carry
        off = pl.multiple_of(r * LANES, LANES)
        lf = lf_ref[pl.ds(off, LANES), :]
        cc = jnp.dot(lower, lf, preferred_element_type=F32, precision=lax.Precision.HIGHEST) + carry_row
        cc_ref[pl.ds(off, LANES), :] = cc
        lf_row = lax.dot_general(pick, lf, _NT, preferred_element_type=F32, precision=lax.Precision.HIGHEST)
        cr = jnp.dot(lf_row, upper, preferred_element_type=F32, precision=lax.Precision.HIGHEST) + carry_col
        cr_ref[:, pl.ds(off, LANES)] = cr
        return cc[LANES - 1:LANES, :], cr[:, LANES - 1:LANES]

    lax.fori_loop(0, n_blocks, body, (jnp.zeros((1, LANES), F32), jnp.zeros((8, 1), F32)))


def _cum(lf_pad, *, nb, skv):
    return pl.pallas_call(
        functools.partial(_cum_kernel, n_blocks=skv // LANES),
        grid=(nb,),
        in_specs=[pl.BlockSpec((skv, LANES), lambda b: (b, 0))],
        out_specs=[
            pl.BlockSpec((skv, LANES), lambda b: (b, 0)),
            pl.BlockSpec((None, 8, skv), lambda b: (b, 0, 0)),
        ],
        out_shape=[
            jax.ShapeDtypeStruct((nb * skv, LANES), F32),
            jax.ShapeDtypeStruct((nb, 8, skv), F32),
        ],
        compiler_params=_params(("arbitrary",)),
        name="fox_cum",
    )(lf_pad)


def _sweep(lo, hi, body, carry, static):
    if static:
        for j in range(lo, hi):
            carry = body(j, carry)
        return carry
    return lax.fori_loop(lo, hi, body, carry)


def _block_offset(j, tk, static):
    return j * tk if static else pl.multiple_of(j * tk, tk)


def _fox_kernel(q_ref, k_ref, v_ref, cq_ref, ck_ref, o_ref, *, q_pos0, tq, tk, heads, single_q):
    hg = pl.program_id(1)
    qpos0 = q_pos0 if single_q else q_pos0 + pl.program_id(2) * tq
    n_full = (qpos0 + 1) // tk
    n_all = (qpos0 + tq - 1) // tk + 1
    scale = HEAD_DIM ** -0.5
    lane = lax.broadcasted_iota(jnp.int32, (tq, LANES), 1)
    cq_all = cq_ref[...]
    qs = [q_ref[:, gi * LANES:(gi + 1) * LANES] for gi in range(heads)]
    cqs = [jnp.sum(jnp.where(lane == hg * heads + gi, cq_all, 0.0), axis=-1, keepdims=True) for gi in range(heads)]

    def step(j, carry, masked):
        off = _block_offset(j, tk, single_q)
        out = []
        for gi in range(heads):
            m, l, acc = carry[gi]
            kj = k_ref[pl.ds(off, tk), gi * LANES:(gi + 1) * LANES]
            vj = v_ref[pl.ds(off, tk), gi * LANES:(gi + 1) * LANES]
            s = lax.dot_general(qs[gi], kj, _NT, preferred_element_type=F32) * scale
            s = s + (cqs[gi] - ck_ref[gi, :, pl.ds(off, tk)])
            if masked:
                qp = qpos0 + lax.broadcasted_iota(jnp.int32, (tq, tk), 0)
                kp = off + lax.broadcasted_iota(jnp.int32, (tq, tk), 1)
                s = jnp.where(qp >= kp, s, -jnp.inf)
            m_new = jnp.maximum(m, jnp.max(s, axis=-1, keepdims=True))
            a = jnp.exp(m - m_new)
            p = jnp.exp(s - m_new)
            l = a * l + jnp.sum(p, axis=-1, keepdims=True)
            acc = a * acc + jnp.dot(p.astype(BF16), vj, preferred_element_type=F32)
            out.append((m_new, l, acc))
        return tuple(out)

    init = tuple((jnp.full((tq, 1), NEG_BIG, F32), jnp.zeros((tq, 1), F32), jnp.zeros((tq, HEAD_DIM), F32))
                 for _ in range(heads))
    carry = _sweep(0, n_full, functools.partial(step, masked=False), init, single_q)
    carry = _sweep(n_full, n_all, functools.partial(step, masked=True), carry, single_q)
    for gi in range(heads):
        _, l, acc = carry[gi]
        o_ref[:, gi * LANES:(gi + 1) * LANES] = (acc / l).astype(o_ref.dtype)


def _fox(q, k, v, cum_col, cum_row, *, nb, sq, skv, q_pos0, tq, tk, heads):
    nq = sq // tq
    cq_blk0 = q_pos0 // tq
    cq_per_b = skv // tq
    wblk = heads * LANES
    return pl.pallas_call(
        functools.partial(_fox_kernel, q_pos0=q_pos0, tq=tq, tk=tk, heads=heads, single_q=nq == 1),
        grid=(nb, H_A // heads, nq),
        in_specs=[
            pl.BlockSpec((tq, wblk), lambda b, h, i: (b * nq + i, h)),
            pl.BlockSpec((skv, wblk), lambda b, h, i: (b, h)),
            pl.BlockSpec((skv, wblk), lambda b, h, i: (b, h)),
            pl.BlockSpec((tq, LANES), lambda b, h, i: (b * cq_per_b + cq_blk0 + i, 0)),
            pl.BlockSpec((None, heads, 1, skv), lambda b, h, i: (b, h, 0, 0)),
        ],
        out_specs=pl.BlockSpec((tq, wblk), lambda b, h, i: (b * nq + i, h)),
        out_shape=jax.ShapeDtypeStruct((nb * sq, W_A), BF16),
        compiler_params=_params(("arbitrary", "arbitrary", "arbitrary")),
        name="fox_attn",
    )(q, k, v, cum_col, cum_row)


def _sb_kernel(q_ref, k_ref, v_ref, u_ref, o_ref, *, q_pos0, tq, tk, heads, single_q):
    qpos0 = q_pos0 if single_q else q_pos0 + pl.program_id(2) * tq
    n_full = qpos0 // tk
    n_all = (qpos0 + tq - 2) // tk + 1
    scale = HEAD_DIM ** -0.5
    u = u_ref[...]
    qs = [q_ref[:, gi * LANES:(gi + 1) * LANES] for gi in range(heads)]

    def step(r, carry, masked, j_top):
        j = j_top - r
        off = _block_offset(j, tk, single_q)
        out = []
        for gi in range(heads):
            rem_run, acc = carry[gi]
            kj = k_ref[pl.ds(off, tk), gi * LANES:(gi + 1) * LANES]
            vj = v_ref[pl.ds(off, tk), gi * LANES:(gi + 1) * LANES]
            z = lax.dot_general(qs[gi], kj, _NT, preferred_element_type=F32) * scale
            ls = _log_sigmoid(z)
            l1m = ls - z
            if masked:
                qp = qpos0 + lax.broadcasted_iota(jnp.int32, (tq, tk), 0)
                kp = off + lax.broadcasted_iota(jnp.int32, (tq, tk), 1)
                mask = kp < qp
                l1m = jnp.where(mask, l1m, 0.0)
            hi = l1m.astype(BF16)
            lo = (l1m - hi.astype(F32)).astype(BF16)
            rem = (jnp.dot(hi, u, preferred_element_type=F32) + jnp.dot(lo, u, preferred_element_type=F32)
                   + rem_run)
            w = jnp.exp(ls + rem)
            if masked:
                w = jnp.where(mask, w, 0.0)
            acc = acc + jnp.dot(w.astype(BF16), vj, preferred_element_type=F32)
            rem_run = rem_run + jnp.sum(l1m, axis=-1, keepdims=True)
            out.append((rem_run, acc))
        return tuple(out)

    init = tuple((jnp.zeros((tq, 1), F32), jnp.zeros((tq, HEAD_DIM), F32)) for _ in range(heads))
    carry = _sweep(0, n_all - n_full, functools.partial(step, masked=True, j_top=n_all - 1), init, single_q)
    carry = _sweep(0, n_full, functools.partial(step, masked=False, j_top=n_full - 1), carry, single_q)
    for gi in range(heads):
        o_ref[:, gi * LANES:(gi + 1) * LANES] = carry[gi][1].astype(o_ref.dtype)


def _sb(q, k, v, *, nb, sq, skv, q_pos0, tq, tk, heads):
    nq = sq // tq
    wblk = heads * LANES
    r = lax.broadcasted_iota(jnp.int32, (tk, tk), 0)
    c = lax.broadcasted_iota(jnp.int32, (tk, tk), 1)
    u = (r > c).astype(BF16)
    return pl.pallas_call(
        functools.partial(_sb_kernel, q_pos0=q_pos0, tq=tq, tk=tk, heads=heads, single_q=nq == 1),
        grid=(nb, H_C // heads, nq),
        in_specs=[
            pl.BlockSpec((tq, wblk), lambda b, h, i: (b * nq + i, h)),
            pl.BlockSpec((skv, wblk), lambda b, h, i: (b, h)),
            pl.BlockSpec((skv, wblk), lambda b, h, i: (b, h)),
            pl.BlockSpec((tk, tk), lambda b, h, i: (0, 0)),
        ],
        out_specs=pl.BlockSpec((tq, wblk), lambda b, h, i: (b * nq + i, h)),
        out_shape=jax.ShapeDtypeStruct((nb * sq, W_C), BF16),
        compiler_params=_params(("arbitrary", "arbitrary", "arbitrary")),
        name="sb_attn",
    )(q, k, v, u)


def _cross_kernel(q_ref, k_ref, v_ref, o_ref):
    q = q_ref[...]
    k = k_ref[...].astype(BF16)
    v = v_ref[...].astype(BF16)
    s = lax.dot_general(q, k, _NT, preferred_element_type=F32) * (HD_X ** -0.5)
    m = jnp.max(s, axis=-1, keepdims=True)
    p = jnp.exp(s - m)
    l = jnp.sum(p, axis=-1, keepdims=True)
    o = jnp.dot(p.astype(BF16), v, preferred_element_type=F32) / l
    o_ref[...] = o.astype(o_ref.dtype)


def _cross(q, mk, mv, *, nb, sq, tq):
    nq = sq // tq
    return pl.pallas_call(
        _cross_kernel,
        grid=(nb, H_X, nq),
        in_specs=[
            pl.BlockSpec((tq, HD_X), lambda b, h, i: (b * nq + i, h)),
            pl.BlockSpec((N_MEM, HD_X), lambda b, h, i: (b, h)),
            pl.BlockSpec((N_MEM, HD_X), lambda b, h, i: (b, h)),
        ],
        out_specs=pl.BlockSpec((tq, HD_X), lambda b, h, i: (b * nq + i, h)),
        out_shape=jax.ShapeDtypeStruct((nb * sq, D_MODEL), BF16),
        compiler_params=_params(("arbitrary", "arbitrary", "arbitrary")),
        name="cross_attn",
    )(q, mk, mv)


def _hgrn_kernel(q_ref, f_ref, i_ref, g_ref, lb_ref, ng_ref, s0_ref, o_ref, sf_ref, st_ref, *, heads, ts):
    t = pl.program_id(2)
    lbk = HGRN_BLOCK

    @pl.when(t == 0)
    def _():
        for gi in range(heads):
            st_ref[gi] = s0_ref[gi].T

    row = lax.broadcasted_iota(jnp.int32, (lbk, LANES), 0)

    def block(r, carry):
        off = pl.multiple_of(r * lbk, lbk)
        for gi in range(heads):
            cs = slice(gi * LANES, (gi + 1) * LANES)
            xq = q_ref[pl.ds(off, lbk), cs]
            xf = f_ref[pl.ds(off, lbk), cs]
            xi = i_ref[pl.ds(off, lbk), cs]
            xg = g_ref[pl.ds(off, lbk), cs]
            lb = lb_ref[:, cs]
            f = lb + (1.0 - lb) * _sigmoid(xf)
            kk = 1.0 - f
            qq = xq * _sigmoid(xq)
            bcum = jnp.log(f)
            sh = 1
            while sh < lbk:
                bcum = bcum + jnp.where(row >= sh, pltpu.roll(bcum, sh, 0), 0.0)
                sh *= 2
            o = jnp.sum(qq * kk, axis=-1, keepdims=True) * xi
            for d in range(1, lbk):
                e = jnp.exp(bcum - pltpu.roll(bcum, d, 0))
                wgt = jnp.sum(jnp.where(row >= d, qq * pltpu.roll(kk, d, 0) * e, 0.0), axis=-1, keepdims=True)
                o = o + wgt * pltpu.roll(xi, d, 0)
            st = st_ref[gi]
            qe = (qq * jnp.exp(bcum)).astype(BF16)
            o = o + lax.dot_general(qe, st.astype(BF16), _NT, preferred_element_type=F32)
            b_last = bcum[lbk - 1:lbk, :]
            ke = (kk * jnp.exp(b_last - bcum)).astype(BF16)
            st_ref[gi] = st * jnp.exp(b_last) + lax.dot_general(xi.astype(BF16), ke, _TN, preferred_element_type=F32)
            rms = lax.rsqrt(jnp.mean(o * o, axis=-1, keepdims=True) + RMS_EPS)
            gated = o * rms * ng_ref[:, cs] * (xg * _sigmoid(xg))
            o_ref[pl.ds(off, lbk), cs] = gated.astype(o_ref.dtype)
        return carry

    lax.fori_loop(0, ts // lbk, block, 0)

    @pl.when(t == pl.num_programs(2) - 1)
    def _():
        for gi in range(heads):
            sf_ref[gi] = st_ref[gi].T


def _hgrn(hqf, hig, lb, ng, s0, *, nb, s, ts, heads):
    nt = s // ts
    ng_groups = H_B // heads
    wblk = heads * LANES

    def col(seg):
        return lambda b, hg, t: (b * nt + t, seg * ng_groups + hg)

    return pl.pallas_call(
        functools.partial(_hgrn_kernel, heads=heads, ts=ts),
        grid=(nb, ng_groups, nt),
        in_specs=[
            pl.BlockSpec((ts, wblk), col(0)),
            pl.BlockSpec((ts, wblk), col(1)),
            pl.BlockSpec((ts, wblk), col(0)),
            pl.BlockSpec((ts, wblk), col(1)),
            pl.BlockSpec((1, wblk), lambda b, hg, t: (0, hg)),
            pl.BlockSpec((1, wblk), lambda b, hg, t: (0, hg)),
            pl.BlockSpec((None, heads, LANES, LANES), lambda b, hg, t: (b, hg, 0, 0)),
        ],
        out_specs=[
            pl.BlockSpec((ts, wblk), lambda b, hg, t: (b * nt + t, hg)),
            pl.BlockSpec((None, heads, LANES, LANES), lambda b, hg, t: (b, hg, 0, 0)),
        ],
        out_shape=[
            jax.ShapeDtypeStruct((nb * s, W_B), BF16),
            jax.ShapeDtypeStruct((nb, H_B, LANES, LANES), F32),
        ],
        scratch_shapes=[pltpu.VMEM((heads, LANES, LANES), F32)],
        compiler_params=_params(("arbitrary", "arbitrary", "arbitrary")),
        name="hgrn",
    )(hqf, hqf, hig, hig, lb, ng, s0)


def _cat_cache(cache, new, nb, t_new, s_pad):
    p = cache.shape[1]
    width = new.shape[1]
    parts = [cache.reshape(nb, p, width).astype(BF16), new.reshape(nb, t_new, width),
             jnp.zeros((nb, s_pad - p - t_new, width), BF16)]
    return jnp.concatenate(parts, axis=1).reshape(nb * s_pad, width)


def kernel(x_prompt, x_sample, mem_prompt, cache_fox_k, cache_fox_v, cache_fox_logf, state_hgrn, cache_sb_k, cache_sb_v, cache_mem_k, cache_mem_v, ln_g, ln_b, ffn1_w_gate, ffn1_w_up, ffn1_w_down, ffn2_w_gate, ffn2_w_up, ffn2_w_down, x_w_q, x_w_kv, x_w_o, ev_w_in, fox_b_f, hgrn_lb_logits, hgrn_norm_g, ev_w_out, od_w_in, od_w_out):
    nbp, sp, _ = x_prompt.shape
    nbs, ss, _ = x_sample.shape
    past = cache_fox_k.shape[2]
    s_all = ((past + ss + LANES - 1) // LANES) * LANES
    tm_p, tm_s = TM_PROMPT, nbs * ss
    xp = x_prompt.reshape(nbp * sp, D_MODEL)
    xs = x_sample.reshape(nbs * ss, D_MODEL)
    mem = mem_prompt.reshape(nbp * N_MEM, D_MODEL)

    outs = {k: [] for k in ("fk_p", "fv_p", "flf_p", "hs_p", "sk_p", "sv_p", "mk_p", "mv_p",
                            "fk_s", "fv_s", "flf_s", "hs_s", "sk_s", "sv_s")}
    for l in range(DEPTH):
        g = [ln_g[l, k].reshape(1, D_MODEL) for k in range(4)]
        bb = [ln_b[l, k].reshape(1, D_MODEL) for k in range(4)]
        j = l // 2

        w1 = (ffn1_w_gate[l].astype(BF16), ffn1_w_up[l].astype(BF16), ffn1_w_down[l].astype(BF16))
        xp = _ffn(xp, *w1, g[0], bb[0], tm=tm_p)
        xs = _ffn(xs, *w1, g[0], bb[0], tm=tm_s)

        if l % 2 == 0:
            w_in = ev_w_in[j]
            wf = jnp.pad(w_in[:, 3 * W_A:3 * W_A + H_A], ((0, 0), (0, LANES - H_A))).astype(BF16)
            w_hb = w_in[:, 3 * W_A + H_A:]
            bf_pad = jnp.pad(fox_b_f[j].astype(F32), (0, LANES - H_A)).reshape(1, LANES)
            lb = jnp.cumsum(jax.nn.softmax(hgrn_lb_logits.astype(F32), axis=0), axis=0)[j].reshape(1, W_B)
            ng = hgrn_norm_g[j].astype(F32).reshape(1, W_B)
            w_out = ev_w_out[j]

            def even_proj(x, tm, tag):
                qa = _proj(x, w_in, col=0, tn=W_A, tm=tm, kind="bf16", name="ev_q" + tag)
                ka16, ka = _proj(x, w_in, col=1, tn=W_A, tm=tm, kind="kv", name="ev_k" + tag)
                va16, va = _proj(x, w_in, col=2, tn=W_A, tm=tm, kind="kv", name="ev_v" + tag)
                hqf = _proj(x, w_hb, col=0, tn=2 * W_B, tm=tm, kind="f32", name="ev_qf" + tag)
                hig = _proj(x, w_hb, col=1, tn=2 * W_B, tm=tm, kind="f32", name="ev_ig" + tag)
                lf = _logf(x, wf, bf_pad, tm=tm)
                return qa, ka16, ka, va16, va, hqf, hig, lf

            qa, ka16, ka, va16, va, hqf, hig, lf = even_proj(xp, tm_p, "")
            cum_col, cum_row = _cum(lf, nb=nbp, skv=sp)
            oa = _fox(qa, ka16, va16, cum_col, cum_row.reshape(nbp, 8, 1, sp), nb=nbp, sq=sp, skv=sp,
                      q_pos0=0, tq=ATTN_TQ, tk=ATTN_TQ, heads=ATTN_HEADS)
            ob, s_fin = _hgrn(hqf, hig, lb, ng, jnp.zeros((nbp, H_B, LANES, LANES), F32), nb=nbp, s=sp,
                              ts=HGRN_TS, heads=HGRN_HEADS)
            outs["fk_p"].append(ka.reshape(nbp, sp, H_A, HEAD_DIM))
            outs["fv_p"].append(va.reshape(nbp, sp, H_A, HEAD_DIM))
            outs["flf_p"].append(lf[:, :H_A].reshape(nbp, sp, H_A))
            outs["hs_p"].append(s_fin)
            xp = _lin_ln([oa, ob], w_out, xp, g[1], bb[1], tm=tm_p, name="ev_out_ln")

            qa, ka16, ka, va16, va, hqf, hig, lf = even_proj(xs, tm_s, "_s")
            c_lf = jnp.pad(cache_fox_logf[j].astype(F32), ((0, 0), (0, 0), (0, LANES - H_A)))
            lf_all = jnp.concatenate(
                [c_lf, lf.reshape(nbs, ss, LANES), jnp.zeros((nbs, s_all - past - ss, LANES), F32)], axis=1)
            cum_col, cum_row = _cum(lf_all.reshape(nbs * s_all, LANES), nb=nbs, skv=s_all)
            k_all = _cat_cache(cache_fox_k[j], ka16, nbs, ss, s_all)
            v_all = _cat_cache(cache_fox_v[j], va16, nbs, ss, s_all)
            oa = _fox(qa, k_all, v_all, cum_col, cum_row.reshape(nbs, 8, 1, s_all), nb=nbs, sq=ss, skv=s_all,
                      q_pos0=past, tq=ss, tk=LANES, heads=H_A)
            ob, s_new = _hgrn(hqf, hig, lb, ng, state_hgrn[j].astype(F32), nb=nbs, s=ss, ts=ss, heads=HGRN_HEADS)
            outs["fk_s"].append(ka.reshape(nbs, ss, H_A, HEAD_DIM))
            outs["fv_s"].append(va.reshape(nbs, ss, H_A, HEAD_DIM))
            outs["flf_s"].append(lf[:, :H_A].reshape(nbs, ss, H_A))
            outs["hs_s"].append(s_new)
            xs = _lin_ln([oa, ob], w_out, xs, g[1], bb[1], tm=tm_s, name="ev_out_ln_s")
        else:
            w_in = od_w_in[j]
            w_out = od_w_out[j]

            def odd_proj(x, tm, tag):
                q = _proj(x, w_in, col=0, tn=W_C, tm=tm, kind="bf16", name="od_q" + tag)
                k16, k = _proj(x, w_in, col=1, tn=W_C, tm=tm, kind="kv", name="od_k" + tag)
                v16, v = _proj(x, w_in, col=2, tn=W_C, tm=tm, kind="kv", name="od_v" + tag)
                return q, k16, k, v16, v

            q, k16, k, v16, v = odd_proj(xp, tm_p, "")
            o = _sb(q, k16, v16, nb=nbp, sq=sp, skv=sp, q_pos0=0, tq=ATTN_TQ, tk=ATTN_TQ, heads=ATTN_HEADS)
            outs["sk_p"].append(k.reshape(nbp, sp, H_C, HEAD_DIM))
            outs["sv_p"].append(v.reshape(nbp, sp, H_C, HEAD_DIM))
            xp = _lin_ln([o], w_out, xp, g[1], bb[1], tm=tm_p, name="od_out_ln")

            q, k16, k, v16, v = odd_proj(xs, tm_s, "_s")
            k_all = _cat_cache(cache_sb_k[j], k16, nbs, ss, s_all)
            v_all = _cat_cache(cache_sb_v[j], v16, nbs, ss, s_all)
            o = _sb(q, k_all, v_all, nb=nbs, sq=ss, skv=s_all, q_pos0=past, tq=ss, tk=LANES, heads=H_C // 2)
            outs["sk_s"].append(k.reshape(nbs, ss, H_C, HEAD_DIM))
            outs["sv_s"].append(v.reshape(nbs, ss, H_C, HEAD_DIM))
            xs = _lin_ln([o], w_out, xs, g[1], bb[1], tm=tm_s, name="od_out_ln_s")

        mk = _proj(mem, x_w_kv[l], col=0, tn=D_MODEL, tm=TM_PROMPT, kind="f32", name="mem_k")
        mv = _proj(mem, x_w_kv[l], col=1, tn=D_MODEL, tm=TM_PROMPT, kind="f32", name="mem_v")
        outs["mk_p"].append(mk.reshape(nbp, N_MEM, H_X, HD_X))
        outs["mv_p"].append(mv.reshape(nbp, N_MEM, H_X, HD_X))
        qx = _proj(xp, x_w_q[l], col=0, tn=D_MODEL, tm=tm_p, kind="bf16", name="x_q")
        ox = _cross(qx, mk, mv, nb=nbp, sq=sp, tq=TM_PROMPT)
        xp = _lin_ln([ox], x_w_o[l], xp, g[2], bb[2], tm=tm_p, name="x_out_ln")
        qx = _proj(xs, x_w_q[l], col=0, tn=D_MODEL, tm=tm_s, kind="bf16", name="x_q_s")
        ox = _cross(qx, cache_mem_k[l].reshape(nbs * N_MEM, D_MODEL).astype(F32),
                    cache_mem_v[l].reshape(nbs * N_MEM, D_MODEL).astype(F32), nb=nbs, sq=ss, tq=ss)
        xs = _lin_ln([ox], x_w_o[l], xs, g[2], bb[2], tm=tm_s, name="x_out_ln_s")

        w2 = (ffn2_w_gate[l].astype(BF16), ffn2_w_up[l].astype(BF16), ffn2_w_down[l].astype(BF16))
        xp = _ffn(xp, *w2, g[3], bb[3], tm=tm_p)
        xs = _ffn(xs, *w2, g[3], bb[3], tm=tm_s)

    st = {k: jnp.stack(v) for k, v in outs.items()}
    return (xp.reshape(nbp, sp, D_MODEL), xs.reshape(nbs, ss, D_MODEL),
            st["fk_p"], st["fv_p"], st["flf_p"], st["hs_p"], st["sk_p"], st["sv_p"], st["mk_p"], st["mv_p"],
            st["fk_s"], st["fv_s"], st["flf_s"], st["hs_s"], st["sk_s"], st["sv_s"])
```

```python
import functools

import jax
import jax.numpy as jnp
from jax import lax
from jax.experimental import pallas as pl
from jax.experimental.pallas import tpu as pltpu

F32 = jnp.float32
BF16 = jnp.bfloat16

D_MODEL = 2048
DEPTH = 2
HEAD_DIM = 128
H_A = D_MODEL // (2 * HEAD_DIM)
W_A = H_A * HEAD_DIM
H_B = D_MODEL // (2 * HEAD_DIM)
W_B = H_B * 128
H_C = D_MODEL // HEAD_DIM
W_C = H_C * HEAD_DIM
H_X = 4
HD_X = D_MODEL // H_X
N_MEM = 256
D_FF = ((8 * D_MODEL // 3 + 127) // 128) * 128
ALPHA = (2.0 * DEPTH) ** 0.25
LN_EPS = 1e-5
RMS_EPS = 1e-6

LANES = 128
VMEM_LIMIT_BYTES = 56 * 1024 * 1024
FFN_MAIN = 768
FFN_TAIL = 128
FFN_N_MAIN = D_FF // FFN_MAIN
FFN_N_TAIL = (D_FF - FFN_N_MAIN * FFN_MAIN) // FFN_TAIL
FFN_TAIL_BLOCK0 = FFN_N_MAIN * FFN_MAIN // FFN_TAIL
assert FFN_N_MAIN * FFN_MAIN + FFN_N_TAIL * FFN_TAIL == D_FF
TM_PROMPT = 512
ATTN_TQ = 256
ATTN_HEADS = 2
HGRN_BLOCK = 16
HGRN_HEADS = 4
HGRN_TS = 512
NEG_BIG = -1e30

_NT = (((1,), (1,)), ((), ()))
_TN = (((0,), (0,)), ((), ()))


def _params(sem):
    return pltpu.CompilerParams(dimension_semantics=sem, vmem_limit_bytes=VMEM_LIMIT_BYTES)


def _resident(shape, index_map):
    return pl.BlockSpec(shape, index_map, pipeline_mode=pl.Buffered(1))


def _layer_norm(z, g, b):
    mu = jnp.mean(z, axis=-1, keepdims=True)
    zc = z - mu
    var = jnp.mean(zc * zc, axis=-1, keepdims=True)
    return zc * lax.rsqrt(var + LN_EPS) * g + b


def _sigmoid(x):
    return 1.0 / (1.0 + jnp.exp(-x))


def _log_sigmoid(x):
    return jnp.minimum(x, 0.0) - jnp.log(1.0 + jnp.exp(-jnp.abs(x)))


def _ffn_kernel(x_ref, wg_ref, wu_ref, wd_ref, wgt_ref, wut_ref, wdt_ref, g_ref, b_ref, y_ref, xb_ref):
    j = pl.program_id(1)

    @pl.when(j == 0)
    def _():
        xb_ref[...] = x_ref[...].astype(BF16)
        y_ref[...] = jnp.zeros_like(y_ref)

    def accumulate(wg, wu, wd):
        xb = xb_ref[...]
        gate = jnp.dot(xb, wg[...], preferred_element_type=F32)
        up = jnp.dot(xb, wu[...], preferred_element_type=F32)
        h = (gate * _sigmoid(gate) * up).astype(BF16)
        y_ref[...] += jnp.dot(h, wd[...], preferred_element_type=F32)

    @pl.when(j < FFN_N_MAIN)
    def _():
        accumulate(wg_ref, wu_ref, wd_ref)

    @pl.when(j >= FFN_N_MAIN)
    def _():
        accumulate(wgt_ref, wut_ref, wdt_ref)

    @pl.when(j == pl.num_programs(1) - 1)
    def _():
        z = ALPHA * x_ref[...] + 0.5 * y_ref[...]
        y_ref[...] = _layer_norm(z, g_ref[...], b_ref[...])


def _ffn(x, wg, wu, wd, layer, g, b, *, tm):
    m = x.shape[0]

    def main(j):
        return jnp.minimum(j, FFN_N_MAIN - 1)

    def tail(j):
        return FFN_TAIL_BLOCK0 + jnp.maximum(j - FFN_N_MAIN, 0)

    return pl.pallas_call(
        _ffn_kernel,
        grid=(m // tm, FFN_N_MAIN + FFN_N_TAIL),
        in_specs=[
            pl.BlockSpec((tm, D_MODEL), lambda i, j: (i, 0)),
            pl.BlockSpec((None, D_MODEL, FFN_MAIN), lambda i, j: (layer, 0, main(j))),
            pl.BlockSpec((None, D_MODEL, FFN_MAIN), lambda i, j: (layer, 0, main(j))),
            pl.BlockSpec((None, FFN_MAIN, D_MODEL), lambda i, j: (layer, main(j), 0)),
            pl.BlockSpec((None, D_MODEL, FFN_TAIL), lambda i, j: (layer, 0, tail(j))),
            pl.BlockSpec((None, D_MODEL, FFN_TAIL), lambda i, j: (layer, 0, tail(j))),
            pl.BlockSpec((None, FFN_TAIL, D_MODEL), lambda i, j: (layer, tail(j), 0)),
            pl.BlockSpec((1, D_MODEL), lambda i, j: (0, 0)),
            pl.BlockSpec((1, D_MODEL), lambda i, j: (0, 0)),
        ],
        out_specs=pl.BlockSpec((tm, D_MODEL), lambda i, j: (i, 0)),
        out_shape=jax.ShapeDtypeStruct((m, D_MODEL), F32),
        scratch_shapes=[pltpu.VMEM((tm, D_MODEL), BF16)],
        compiler_params=_params(("arbitrary", "arbitrary")),
        name="ffn_ln",
    )(x, wg, wu, wd, wg, wu, wd, g, b)


def _proj_kernel(x_ref, w_ref, *refs, kind, n_heads, tm):
    outs, wb_ref = refs[:-1], refs[-1]

    @pl.when(pl.program_id(0) == 0)
    def _():
        wb_ref[...] = w_ref[...].astype(BF16)

    y = jnp.dot(x_ref[...].astype(BF16), wb_ref[...], preferred_element_type=F32)
    if kind == "kv":
        outs[0][...] = y.astype(BF16)
        for h in range(n_heads):
            outs[1][pl.ds(h, tm, stride=n_heads), :] = y[:, h * LANES:(h + 1) * LANES]
    else:
        outs[0][...] = y.astype(outs[0].dtype)


def _proj(x, w, layer, *, col, tn, tm, kind, name):
    m, k = x.shape
    n_heads = tn // LANES
    if kind == "kv":
        out_specs = [pl.BlockSpec((tm, tn), lambda i: (i, 0)),
                     pl.BlockSpec((tm * n_heads, LANES), lambda i: (i, 0))]
        out_shape = [jax.ShapeDtypeStruct((m, tn), BF16), jax.ShapeDtypeStruct((m * n_heads, LANES), F32)]
    else:
        out_specs = pl.BlockSpec((tm, tn), lambda i: (i, 0))
        out_shape = jax.ShapeDtypeStruct((m, tn), BF16 if kind == "bf16" else F32)
    return pl.pallas_call(
        functools.partial(_proj_kernel, kind=kind, n_heads=n_heads, tm=tm),
        grid=(m // tm,),
        in_specs=[
            pl.BlockSpec((tm, k), lambda i: (i, 0)),
            _resident((None, k, tn), lambda i: (layer, 0, col)),
        ],
        out_specs=out_specs,
        out_shape=out_shape,
        scratch_shapes=[pltpu.VMEM((k, tn), BF16)],
        compiler_params=_params(("arbitrary",)),
        name=name,
    )(x, w)


def _logf_kernel(x_ref, w_ref, bf_ref, o_ref):
    fa = jnp.dot(x_ref[...].astype(BF16), w_ref[...], preferred_element_type=F32)
    lane = lax.broadcasted_iota(jnp.int32, fa.shape, 1)
    o_ref[...] = jnp.where(lane < H_A, _log_sigmoid(fa + bf_ref[...]), 0.0)


def _logf(x, w_pad, bf_pad, *, tm):
    m = x.shape[0]
    return pl.pallas_call(
        _logf_kernel,
        grid=(m // tm,),
        in_specs=[
            pl.BlockSpec((tm, D_MODEL), lambda i: (i, 0)),
            pl.BlockSpec((D_MODEL, LANES), lambda i: (0, 0)),
            pl.BlockSpec((1, LANES), lambda i: (0, 0)),
        ],
        out_specs=pl.BlockSpec((tm, LANES), lambda i: (i, 0)),
        out_shape=jax.ShapeDtypeStruct((m, LANES), F32),
        compiler_params=_params(("arbitrary",)),
        name="fox_logf",
    )(x, w_pad, bf_pad)


def _lin_ln_kernel(*refs, n_in):
    a_refs = refs[:n_in]
    w_refs = refs[n_in:2 * n_in]
    x_ref, g_ref, b_ref, y_ref = refs[2 * n_in:2 * n_in + 4]
    wb_refs = refs[2 * n_in + 4:]

    @pl.when(pl.program_id(0) == 0)
    def _():
        for w_ref, wb_ref in zip(w_refs, wb_refs):
            wb_ref[...] = w_ref[...].astype(BF16)

    acc = None
    for a_ref, wb_ref in zip(a_refs, wb_refs):
        part = jnp.dot(a_ref[...], wb_ref[...], preferred_element_type=F32)
        acc = part if acc is None else acc + part
    z = ALPHA * x_ref[...] + acc
    y_ref[...] = _layer_norm(z, g_ref[...], b_ref[...])


def _lin_ln(a_list, w, layer, x, g, b, *, tm, name):
    m = x.shape[0]
    n_in = len(a_list)
    rows = [a.shape[1] for a in a_list]
    assert sum(rows) == w.shape[1] and len(set(rows)) == 1
    in_specs = [pl.BlockSpec((tm, r), lambda i: (i, 0)) for r in rows]
    in_specs += [_resident((None, r, D_MODEL), functools.partial(lambda i, kk: (layer, kk, 0), kk=kk))
                 for kk, r in enumerate(rows)]
    in_specs += [
        pl.BlockSpec((tm, D_MODEL), lambda i: (i, 0)),
        pl.BlockSpec((1, D_MODEL), lambda i: (0, 0)),
        pl.BlockSpec((1, D_MODEL), lambda i: (0, 0)),
    ]
    return pl.pallas_call(
        functools.partial(_lin_ln_kernel, n_in=n_in),
        grid=(m // tm,),
        in_specs=in_specs,
        out_specs=pl.BlockSpec((tm, D_MODEL), lambda i: (i, 0)),
        out_shape=jax.ShapeDtypeStruct((m, D_MODEL), F32),
        scratch_shapes=[pltpu.VMEM((r, D_MODEL), BF16) for r in rows],
        compiler_params=_params(("arbitrary",)),
        name=name,
    )(*a_list, *([w] * n_in), x, g, b)


def _cum_kernel(lf_ref, cc_ref, cr_ref, *, n_blocks):
    r_i = lax.broadcasted_iota(jnp.int32, (LANES, LANES), 0)
    c_i = lax.broadcasted_iota(jnp.int32, (LANES, LANES), 1)
    lower = (r_i >= c_i).astype(F32)
    upper = (r_i <= c_i).astype(F32)
    e_r = lax.broadcasted_iota(jnp.int32, (8, LANES), 0)
    e_c = lax.broadcasted_iota(jnp.int32, (8, LANES), 1)
    pick = (e_r == e_c).astype(F32)

    def body(r, carry):
        carry_row, carry_col = carry
        off = pl.multiple_of(r * LANES, LANES)
        lf = lf_ref[pl.ds(off, LANES), :]
        cc = jnp.dot(lower, lf, preferred_element_type=F32, precision=lax.Precision.HIGHEST) + carry_row
        cc_ref[pl.ds(off, LANES), :] = cc
        lf_row = lax.dot_general(pick, lf, _NT, preferred_element_type=F32, precision=lax.Precision.HIGHEST)
        cr = jnp.dot(lf_row, upper, preferred_element_type=F32, precision=lax.Precision.HIGHEST) + carry_col
        cr_ref[:, pl.ds(off, LANES)] = cr
        return cc[LANES - 1:LANES, :], cr[:, LANES - 1:LANES]

    lax.fori_loop(0, n_blocks, body, (jnp.zeros((1, LANES), F32), jnp.zeros((8, 1), F32)))


def _cum(lf_pad, *, nb, skv):
    return pl.pallas_call(
        functools.partial(_cum_kernel, n_blocks=skv // LANES),
        grid=(nb,),
        in_specs=[pl.BlockSpec((skv, LANES), lambda b: (b, 0))],
        out_specs=[
            pl.BlockSpec((skv, LANES), lambda b: (b, 0)),
            pl.BlockSpec((None, 8, skv), lambda b: (b, 0, 0)),
        ],
        out_shape=[
            jax.ShapeDtypeStruct((nb * skv, LANES), F32),
            jax.ShapeDtypeStruct((nb, 8, skv), F32),
        ],
        compiler_params=_params(("arbitrary",)),
        name="fox_cum",
    )(lf_pad)


def _sweep(lo, hi, body, carry, static):
    if static:
        for j in range(lo, hi):
            carry = body(j, carry)
        return carry
    return lax.fori_loop(lo, hi, body, carry)


def _block_offset(j, tk, static):
    return j * tk if static else pl.multiple_of(j * tk, tk)


def _gather_cache(dst_ref, cache_ref, new_ref, *, head0, heads, n_heads, past, t_new):
    s_all = dst_ref.shape[1]
    for gi in range(heads):
        rows = cache_ref[pl.ds(head0 + gi, past, stride=n_heads), :]
        dst_ref[gi, pl.ds(0, past), :] = rows.astype(BF16)
        dst_ref[gi, pl.ds(past, t_new), :] = new_ref[:, gi * LANES:(gi + 1) * LANES]
        dst_ref[gi, pl.ds(past + t_new, s_all - past - t_new), :] = jnp.zeros((s_all - past - t_new, LANES), BF16)


def _kv_getters(refs, *, tk, heads, cache):
    if cache is None:
        k_ref, v_ref = refs[:2]
        rest = refs[2:]
        return (rest,
                lambda gi, off: k_ref[pl.ds(off, tk), gi * LANES:(gi + 1) * LANES],
                lambda gi, off: v_ref[pl.ds(off, tk), gi * LANES:(gi + 1) * LANES])
    kc_ref, vc_ref, kn_ref, vn_ref = refs[:4]
    ks_ref, vs_ref = refs[-2:]
    head0 = pl.program_id(1) * heads
    _gather_cache(ks_ref, kc_ref, kn_ref, head0=head0, heads=heads, **cache)
    _gather_cache(vs_ref, vc_ref, vn_ref, head0=head0, heads=heads, **cache)
    return (refs[4:-2],
            lambda gi, off: ks_ref[gi, pl.ds(off, tk), :],
            lambda gi, off: vs_ref[gi, pl.ds(off, tk), :])


def _kv_specs(k, v, *, skv, wblk, heads, n_heads, cache):
    if cache is None:
        spec = pl.BlockSpec((skv, wblk), lambda b, h, i: (b, h))
        return [k, v], [spec, spec], []
    cache_k, cache_v, layer, past = cache
    nb = cache_k.shape[1]
    t_new = k.shape[0] // nb
    flat = lambda c: c.reshape(c.shape[0], nb, past * n_heads, LANES)
    c_spec = pl.BlockSpec((None, None, past * n_heads, LANES), lambda b, h, i: (layer, b, 0, 0))
    n_spec = pl.BlockSpec((t_new, wblk), lambda b, h, i: (b, h))
    scratch = [pltpu.VMEM((heads, skv, LANES), BF16), pltpu.VMEM((heads, skv, LANES), BF16)]
    return [flat(cache_k), flat(cache_v), k, v], [c_spec, c_spec, n_spec, n_spec], scratch


def _fox_kernel(q_ref, *refs, q_pos0, tq, tk, heads, single_q, cache):
    (cq_ref, ck_ref, o_ref), k_get, v_get = _kv_getters(refs, tk=tk, heads=heads, cache=cache)
    hg = pl.program_id(1)
    qpos0 = q_pos0 if single_q else q_pos0 + pl.program_id(2) * tq
    n_full = (qpos0 + 1) // tk
    n_all = (qpos0 + tq - 1) // tk + 1
    scale = HEAD_DIM ** -0.5
    lane = lax.broadcasted_iota(jnp.int32, (tq, LANES), 1)
    cq_all = cq_ref[...]
    qs = [q_ref[:, gi * LANES:(gi + 1) * LANES] for gi in range(heads)]
    cqs = [jnp.sum(jnp.where(lane == hg * heads + gi, cq_all, 0.0), axis=-1, keepdims=True) for gi in range(heads)]

    def step(j, carry, masked):
        off = _block_offset(j, tk, single_q)
        out = []
        for gi in range(heads):
            m, l, acc = carry[gi]
            kj = k_get(gi, off)
            vj = v_get(gi, off)
            s = lax.dot_general(qs[gi], kj, _NT, preferred_element_type=F32) * scale
            s = s + (cqs[gi] - ck_ref[gi, :, pl.ds(off, tk)])
            if masked:
                qp = qpos0 + lax.broadcasted_iota(jnp.int32, (tq, tk), 0)
                kp = off + lax.broadcasted_iota(jnp.int32, (tq, tk), 1)
                s = jnp.where(qp >= kp, s, -jnp.inf)
            m_new = jnp.maximum(m, jnp.max(s, axis=-1, keepdims=True))
            a = jnp.exp(m - m_new)
            p = jnp.exp(s - m_new)
            l = a * l + jnp.sum(p, axis=-1, keepdims=True)
            acc = a * acc + jnp.dot(p.astype(BF16), vj, preferred_element_type=F32)
            out.append((m_new, l, acc))
        return tuple(out)

    init = tuple((jnp.full((tq, 1), NEG_BIG, F32), jnp.zeros((tq, 1), F32), jnp.zeros((tq, HEAD_DIM), F32))
                 for _ in range(heads))
    carry = _sweep(0, n_full, functools.partial(step, masked=False), init, single_q)
    carry = _sweep(n_full, n_all, functools.partial(step, masked=True), carry, single_q)
    for gi in range(heads):
        _, l, acc = carry[gi]
        o_ref[:, gi * LANES:(gi + 1) * LANES] = (acc / l).astype(o_ref.dtype)


def _fox(q, k, v, cum_col, cum_row, *, nb, sq, skv, q_pos0, tq, tk, heads, cache=None):
    nq = sq // tq
    cq_blk0 = q_pos0 // tq
    cq_per_b = skv // tq
    wblk = heads * LANES
    kv_ops, kv_specs, scratch = _kv_specs(k, v, skv=skv, wblk=wblk, heads=heads, n_heads=H_A, cache=cache)
    cache_cfg = None if cache is None else dict(n_heads=H_A, past=cache[3], t_new=k.shape[0] // nb)
    return pl.pallas_call(
        functools.partial(_fox_kernel, q_pos0=q_pos0, tq=tq, tk=tk, heads=heads, single_q=nq == 1, cache=cache_cfg),
        grid=(nb, H_A // heads, nq),
        in_specs=[pl.BlockSpec((tq, wblk), lambda b, h, i: (b * nq + i, h))] + kv_specs + [
            pl.BlockSpec((tq, LANES), lambda b, h, i: (b * cq_per_b + cq_blk0 + i, 0)),
            pl.BlockSpec((None, heads, 1, skv), lambda b, h, i: (b, h, 0, 0)),
        ],
        out_specs=pl.BlockSpec((tq, wblk), lambda b, h, i: (b * nq + i, h)),
        out_shape=jax.ShapeDtypeStruct((nb * sq, W_A), BF16),
        scratch_shapes=scratch,
        compiler_params=_params(("arbitrary", "arbitrary", "arbitrary")),
        name="fox_attn",
    )(q, *kv_ops, cum_col, cum_row)


def _sb_kernel(q_ref, *refs, q_pos0, tq, tk, heads, single_q, cache):
    (u_ref, o_ref), k_get, v_get = _kv_getters(refs, tk=tk, heads=heads, cache=cache)
    qpos0 = q_pos0 if single_q else q_pos0 + pl.program_id(2) * tq
    n_full = qpos0 // tk
    n_all = (qpos0 + tq - 2) // tk + 1
    scale = HEAD_DIM ** -0.5
    u = u_ref[...]
    qs = [q_ref[:, gi * LANES:(gi + 1) * LANES] for gi in range(heads)]

    def step(r, carry, masked, j_top):
        j = j_top - r
        off = _block_offset(j, tk, single_q)
        out = []
        for gi in range(heads):
            rem_run, acc = carry[gi]
            kj = k_get(gi, off)
            vj = v_get(gi, off)
            z = lax.dot_general(qs[gi], kj, _NT, preferred_element_type=F32) * scale
            ls = _log_sigmoid(z)
            l1m = ls - z
            if masked:
                qp = qpos0 + lax.broadcasted_iota(jnp.int32, (tq, tk), 0)
                kp = off + lax.broadcasted_iota(jnp.int32, (tq, tk), 1)
                mask = kp < qp
                l1m = jnp.where(mask, l1m, 0.0)
            hi = l1m.astype(BF16)
            lo = (l1m - hi.astype(F32)).astype(BF16)
            rem = (jnp.dot(hi, u, preferred_element_type=F32) + jnp.dot(lo, u, preferred_element_type=F32)
                   + rem_run)
            w = jnp.exp(ls + rem)
            if masked:
                w = jnp.where(mask, w, 0.0)
            acc = acc + jnp.dot(w.astype(BF16), vj, preferred_element_type=F32)
            rem_run = rem_run + jnp.sum(l1m, axis=-1, keepdims=True)
            out.append((rem_run, acc))
        return tuple(out)

    init = tuple((jnp.zeros((tq, 1), F32), jnp.zeros((tq, HEAD_DIM), F32)) for _ in range(heads))
    carry = _sweep(0, n_all - n_full, functools.partial(step, masked=True, j_top=n_all - 1), init, single_q)
    carry = _sweep(0, n_full, functools.partial(step, masked=False, j_top=n_full - 1), carry, single_q)
    for gi in range(heads):
        o_ref[:, gi * LANES:(gi + 1) * LANES] = carry[gi][1].astype(o_ref.dtype)


def _sb(q, k, v, *, nb, sq, skv, q_pos0, tq, tk, heads, cache=None):
    nq = sq // tq
    wblk = heads * LANES
    r = lax.broadcasted_iota(jnp.int32, (tk, tk), 0)
    c = lax.broadcasted_iota(jnp.int32, (tk, tk), 1)
    u = (r > c).astype(BF16)
    kv_ops, kv_specs, scratch = _kv_specs(k, v, skv=skv, wblk=wblk, heads=heads, n_heads=H_C, cache=cache)
    cache_cfg = None if cache is None else dict(n_heads=H_C, past=cache[3], t_new=k.shape[0] // nb)
    return pl.pallas_call(
        functools.partial(_sb_kernel, q_pos0=q_pos0, tq=tq, tk=tk, heads=heads, single_q=nq == 1, cache=cache_cfg),
        grid=(nb, H_C // heads, nq),
        in_specs=[pl.BlockSpec((tq, wblk), lambda b, h, i: (b * nq + i, h))] + kv_specs + [
            pl.BlockSpec((tk, tk), lambda b, h, i: (0, 0)),
        ],
        out_specs=pl.BlockSpec((tq, wblk), lambda b, h, i: (b * nq + i, h)),
        out_shape=jax.ShapeDtypeStruct((nb * sq, W_C), BF16),
        scratch_shapes=scratch,
        compiler_params=_params(("arbitrary", "arbitrary", "arbitrary")),
        name="sb_attn",
    )(q, *kv_ops, u)


def _cross_kernel(q_ref, k_ref, v_ref, o_ref):
    q = q_ref[...]
    k = k_ref[...].astype(BF16)
    v = v_ref[...].astype(BF16)
    s = lax.dot_general(q, k, _NT, preferred_element_type=F32) * (HD_X ** -0.5)
    m = jnp.max(s, axis=-1, keepdims=True)
    p = jnp.exp(s - m)
    l = jnp.sum(p, axis=-1, keepdims=True)
    o = jnp.dot(p.astype(BF16), v, preferred_element_type=F32) / l
    o_ref[...] = o.astype(o_ref.dtype)


def _cross(q, mk, mv, *, nb, sq, tq):
    nq = sq // tq
    return pl.pallas_call(
        _cross_kernel,
        grid=(nb, H_X, nq),
        in_specs=[
            pl.BlockSpec((tq, HD_X), lambda b, h, i: (b * nq + i, h)),
            pl.BlockSpec((N_MEM, HD_X), lambda b, h, i: (b, h)),
            pl.BlockSpec((N_MEM, HD_X), lambda b, h, i: (b, h)),
        ],
        out_specs=pl.BlockSpec((tq, HD_X), lambda b, h, i: (b * nq + i, h)),
        out_shape=jax.ShapeDtypeStruct((nb * sq, D_MODEL), BF16),
        compiler_params=_params(("arbitrary", "arbitrary", "arbitrary")),
        name="cross_attn",
    )(q, mk, mv)


def _hgrn_kernel(q_ref, f_ref, i_ref, g_ref, lb_ref, ng_ref, s0_ref, o_ref, sf_ref, st_ref, *, heads, ts):
    t = pl.program_id(2)
    lbk = HGRN_BLOCK

    @pl.when(t == 0)
    def _():
        for gi in range(heads):
            st_ref[gi] = s0_ref[gi].T

    row = lax.broadcasted_iota(jnp.int32, (lbk, LANES), 0)

    def block(r, carry):
        off = pl.multiple_of(r * lbk, lbk)
        for gi in range(heads):
            cs = slice(gi * LANES, (gi + 1) * LANES)
            xq = q_ref[pl.ds(off, lbk), cs]
            xf = f_ref[pl.ds(off, lbk), cs]
            xi = i_ref[pl.ds(off, lbk), cs]
            xg = g_ref[pl.ds(off, lbk), cs]
            lb = lb_ref[:, cs]
            f = lb + (1.0 - lb) * _sigmoid(xf)
            kk = 1.0 - f
            qq = xq * _sigmoid(xq)
            bcum = jnp.log(f)
            sh = 1
            while sh < lbk:
                bcum = bcum + jnp.where(row >= sh, pltpu.roll(bcum, sh, 0), 0.0)
                sh *= 2
            o = jnp.sum(qq * kk, axis=-1, keepdims=True) * xi
            for d in range(1, lbk):
                e = jnp.exp(bcum - pltpu.roll(bcum, d, 0))
                wgt = jnp.sum(jnp.where(row >= d, qq * pltpu.roll(kk, d, 0) * e, 0.0), axis=-1, keepdims=True)
                o = o + wgt * pltpu.roll(xi, d, 0)
            st = st_ref[gi]
            qe = (qq * jnp.exp(bcum)).astype(BF16)
            o = o + lax.dot_general(qe, st.astype(BF16), _NT, preferred_element_type=F32)
            b_last = bcum[lbk - 1:lbk, :]
            ke = (kk * jnp.exp(b_last - bcum)).astype(BF16)
            st_ref[gi] = st * jnp.exp(b_last) + lax.dot_general(xi.astype(BF16), ke, _TN, preferred_element_type=F32)
            rms = lax.rsqrt(jnp.mean(o * o, axis=-1, keepdims=True) + RMS_EPS)
            gated = o * rms * ng_ref[:, cs] * (xg * _sigmoid(xg))
            o_ref[pl.ds(off, lbk), cs] = gated.astype(o_ref.dtype)
        return carry

    lax.fori_loop(0, ts // lbk, block, 0)

    @pl.when(t == pl.num_programs(2) - 1)
    def _():
        for gi in range(heads):
            sf_ref[gi] = st_ref[gi].T


def _hgrn(hqf, hig, lb, ng, s0, *, nb, s, ts, heads):
    nt = s // ts
    ng_groups = H_B // heads
    wblk = heads * LANES

    def col(seg):
        return lambda b, hg, t: (b * nt + t, seg * ng_groups + hg)

    return pl.pallas_call(
        functools.partial(_hgrn_kernel, heads=heads, ts=ts),
        grid=(nb, ng_groups, nt),
        in_specs=[
            pl.BlockSpec((ts, wblk), col(0)),
            pl.BlockSpec((ts, wblk), col(1)),
            pl.BlockSpec((ts, wblk), col(0)),
            pl.BlockSpec((ts, wblk), col(1)),
            pl.BlockSpec((1, wblk), lambda b, hg, t: (0, hg)),
            pl.BlockSpec((1, wblk), lambda b, hg, t: (0, hg)),
            pl.BlockSpec((None, heads, LANES, LANES), lambda b, hg, t: (b, hg, 0, 0)),
        ],
        out_specs=[
            pl.BlockSpec((ts, wblk), lambda b, hg, t: (b * nt + t, hg)),
            pl.BlockSpec((None, heads, LANES, LANES), lambda b, hg, t: (b, hg, 0, 0)),
        ],
        out_shape=[
            jax.ShapeDtypeStruct((nb * s, W_B), BF16),
            jax.ShapeDtypeStruct((nb, H_B, LANES, LANES), F32),
        ],
        scratch_shapes=[pltpu.VMEM((heads, LANES, LANES), F32)],
        compiler_params=_params(("arbitrary", "arbitrary", "arbitrary")),
        name="hgrn",
    )(hqf, hqf, hig, hig, lb, ng, s0)


def kernel(x_prompt, x_sample, mem_prompt, cache_fox_k, cache_fox_v, cache_fox_logf, state_hgrn, cache_sb_k, cache_sb_v, cache_mem_k, cache_mem_v, ln_g, ln_b, ffn1_w_gate, ffn1_w_up, ffn1_w_down, ffn2_w_gate, ffn2_w_up, ffn2_w_down, x_w_q, x_w_kv, x_w_o, ev_w_in, fox_b_f, hgrn_lb_logits, hgrn_norm_g, ev_w_out, od_w_in, od_w_out):
    nbp, sp, _ = x_prompt.shape
    nbs, ss, _ = x_sample.shape
    past = cache_fox_k.shape[2]
    s_all = ((past + ss + LANES - 1) // LANES) * LANES
    tm_p, tm_s = TM_PROMPT, nbs * ss
    xp = x_prompt.reshape(nbp * sp, D_MODEL)
    xs = x_sample.reshape(nbs * ss, D_MODEL)
    mem = mem_prompt.reshape(nbp * N_MEM, D_MODEL)

    outs = {k: [] for k in ("fk_p", "fv_p", "flf_p", "hs_p", "sk_p", "sv_p", "mk_p", "mv_p",
                            "fk_s", "fv_s", "flf_s", "hs_s", "sk_s", "sv_s")}
    w1 = (ffn1_w_gate.astype(BF16), ffn1_w_up.astype(BF16), ffn1_w_down.astype(BF16))
    w2 = (ffn2_w_gate.astype(BF16), ffn2_w_up.astype(BF16), ffn2_w_down.astype(BF16))
    for l in range(DEPTH):
        g = [ln_g[l, k].reshape(1, D_MODEL) for k in range(4)]
        bb = [ln_b[l, k].reshape(1, D_MODEL) for k in range(4)]
        j = l // 2

        xp = _ffn(xp, *w1, l, g[0], bb[0], tm=tm_p)
        xs = _ffn(xs, *w1, l, g[0], bb[0], tm=tm_s)

        if l % 2 == 0:
            w_in = ev_w_in
            wf = jnp.pad(ev_w_in[j][:, 3 * W_A:3 * W_A + H_A], ((0, 0), (0, LANES - H_A))).astype(BF16)
            w_hb = ev_w_in[:, :, 3 * W_A + H_A:]
            bf_pad = jnp.pad(fox_b_f[j].astype(F32), (0, LANES - H_A)).reshape(1, LANES)
            lb = jnp.cumsum(jax.nn.softmax(hgrn_lb_logits.astype(F32), axis=0), axis=0)[j].reshape(1, W_B)
            ng = hgrn_norm_g[j].astype(F32).reshape(1, W_B)
            w_out = ev_w_out

            def even_proj(x, tm, tag):
                qa = _proj(x, w_in, j, col=0, tn=W_A, tm=tm, kind="bf16", name="ev_q" + tag)
                ka16, ka = _proj(x, w_in, j, col=1, tn=W_A, tm=tm, kind="kv", name="ev_k" + tag)
                va16, va = _proj(x, w_in, j, col=2, tn=W_A, tm=tm, kind="kv", name="ev_v" + tag)
                hqf = _proj(x, w_hb, j, col=0, tn=2 * W_B, tm=tm, kind="f32", name="ev_qf" + tag)
                hig = _proj(x, w_hb, j, col=1, tn=2 * W_B, tm=tm, kind="f32", name="ev_ig" + tag)
                lf = _logf(x, wf, bf_pad, tm=tm)
                return qa, ka16, ka, va16, va, hqf, hig, lf

            qa, ka16, ka, va16, va, hqf, hig, lf = even_proj(xp, tm_p, "")
            cum_col, cum_row = _cum(lf, nb=nbp, skv=sp)
            oa = _fox(qa, ka16, va16, cum_col, cum_row.reshape(nbp, 8, 1, sp), nb=nbp, sq=sp, skv=sp,
                      q_pos0=0, tq=ATTN_TQ, tk=ATTN_TQ, heads=ATTN_HEADS)
            ob, s_fin = _hgrn(hqf, hig, lb, ng, jnp.zeros((nbp, H_B, LANES, LANES), F32), nb=nbp, s=sp,
                              ts=HGRN_TS, heads=HGRN_HEADS)
            outs["fk_p"].append(ka.reshape(nbp, sp, H_A, HEAD_DIM))
            outs["fv_p"].append(va.reshape(nbp, sp, H_A, HEAD_DIM))
            outs["flf_p"].append(lf[:, :H_A].reshape(nbp, sp, H_A))
            outs["hs_p"].append(s_fin)
            xp = _lin_ln([oa, ob], w_out, j, xp, g[1], bb[1], tm=tm_p, name="ev_out_ln")

            qa, ka16, ka, va16, va, hqf, hig, lf = even_proj(xs, tm_s, "_s")
            c_lf = jnp.pad(cache_fox_logf[j].astype(F32), ((0, 0), (0, 0), (0, LANES - H_A)))
            lf_all = jnp.concatenate(
                [c_lf, lf.reshape(nbs, ss, LANES), jnp.zeros((nbs, s_all - past - ss, LANES), F32)], axis=1)
            cum_col, cum_row = _cum(lf_all.reshape(nbs * s_all, LANES), nb=nbs, skv=s_all)
            oa = _fox(qa, ka16, va16, cum_col, cum_row.reshape(nbs, 8, 1, s_all), nb=nbs, sq=ss, skv=s_all,
                      q_pos0=past, tq=ss, tk=LANES, heads=H_A, cache=(cache_fox_k, cache_fox_v, j, past))
            ob, s_new = _hgrn(hqf, hig, lb, ng, state_hgrn[j].astype(F32), nb=nbs, s=ss, ts=ss, heads=HGRN_HEADS)
            outs["fk_s"].append(ka.reshape(nbs, ss, H_A, HEAD_DIM))
            outs["fv_s"].append(va.reshape(nbs, ss, H_A, HEAD_DIM))
            outs["flf_s"].append(lf[:, :H_A].reshape(nbs, ss, H_A))
            outs["hs_s"].append(s_new)
            xs = _lin_ln([oa, ob], w_out, j, xs, g[1], bb[1], tm=tm_s, name="ev_out_ln_s")
        else:
            w_in = od_w_in
            w_out = od_w_out

            def odd_proj(x, tm, tag):
                q = _proj(x, w_in, j, col=0, tn=W_C, tm=tm, kind="bf16", name="od_q" + tag)
                k16, k = _proj(x, w_in, j, col=1, tn=W_C, tm=tm, kind="kv", name="od_k" + tag)
                v16, v = _proj(x, w_in, j, col=2, tn=W_C, tm=tm, kind="kv", name="od_v" + tag)
                return q, k16, k, v16, v

            q, k16, k, v16, v = odd_proj(xp, tm_p, "")
            o = _sb(q, k16, v16, nb=nbp, sq=sp, skv=sp, q_pos0=0, tq=ATTN_TQ, tk=ATTN_TQ, heads=ATTN_HEADS)
            outs["sk_p"].append(k.reshape(nbp, sp, H_C, HEAD_DIM))
            outs["sv_p"].append(v.reshape(nbp, sp, H_C, HEAD_DIM))
            xp = _lin_ln([o], w_out, j, xp, g[1], bb[1], tm=tm_p, name="od_out_ln")

            q, k16, k, v16, v = odd_proj(xs, tm_s, "_s")
            o = _sb(q, k16, v16, nb=nbs, sq=ss, skv=s_all, q_pos0=past, tq=ss, tk=LANES, heads=H_C // 2,
                    cache=(cache_sb_k, cache_sb_v, j, past))
            outs["sk_s"].append(k.reshape(nbs, ss, H_C, HEAD_DIM))
            outs["sv_s"].append(v.reshape(nbs, ss, H_C, HEAD_DIM))
            xs = _lin_ln([o], w_out, j, xs, g[1], bb[1], tm=tm_s, name="od_out_ln_s")

        mk = _proj(mem, x_w_kv, l, col=0, tn=D_MODEL, tm=TM_PROMPT, kind="f32", name="mem_k")
        mv = _proj(mem, x_w_kv, l, col=1, tn=D_MODEL, tm=TM_PROMPT, kind="f32", name="mem_v")
        outs["mk_p"].append(mk.reshape(nbp, N_MEM, H_X, HD_X))
        outs["mv_p"].append(mv.reshape(nbp, N_MEM, H_X, HD_X))
        qx = _proj(xp, x_w_q, l, col=0, tn=D_MODEL, tm=tm_p, kind="bf16", name="x_q")
        ox = _cross(qx, mk, mv, nb=nbp, sq=sp, tq=TM_PROMPT)
        xp = _lin_ln([ox], x_w_o, l, xp, g[2], bb[2], tm=tm_p, name="x_out_ln")
        qx = _proj(xs, x_w_q, l, col=0, tn=D_MODEL, tm=tm_s, kind="bf16", name="x_q_s")
        ox = _cross(qx, cache_mem_k[l].reshape(nbs * N_MEM, D_MODEL).astype(F32),
                    cache_mem_v[l].reshape(nbs * N_MEM, D_MODEL).astype(F32), nb=nbs, sq=ss, tq=ss)
        xs = _lin_ln([ox], x_w_o, l, xs, g[2], bb[2], tm=tm_s, name="x_out_ln_s")

        xp = _ffn(xp, *w2, l, g[3], bb[3], tm=tm_p)
        xs = _ffn(xs, *w2, l, g[3], bb[3], tm=tm_s)

    st = {k: jnp.stack(v) for k, v in outs.items()}
    return (xp.reshape(nbp, sp, D_MODEL), xs.reshape(nbs, ss, D_MODEL),
            st["fk_p"], st["fv_p"], st["flf_p"], st["hs_p"], st["sk_p"], st["sv_p"], st["mk_p"], st["mv_p"],
            st["fk_s"], st["fv_s"], st["flf_s"], st["hs_s"], st["sk_s"], st["sv_s"])
```

```python
import functools

import jax
import jax.numpy as jnp
from jax import lax
from jax.experimental import pallas as pl
from jax.experimental.pallas import tpu as pltpu

F32 = jnp.float32
BF16 = jnp.bfloat16

D_MODEL = 2048
DEPTH = 2
HEAD_DIM = 128
H_A = D_MODEL // (2 * HEAD_DIM)
W_A = H_A * HEAD_DIM
H_B = D_MODEL // (2 * HEAD_DIM)
W_B = H_B * 128
H_C = D_MODEL // HEAD_DIM
W_C = H_C * HEAD_DIM
H_X = 4
HD_X = D_MODEL // H_X
N_MEM = 256
D_FF = ((8 * D_MODEL // 3 + 127) // 128) * 128
ALPHA = (2.0 * DEPTH) ** 0.25
LN_EPS = 1e-5
RMS_EPS = 1e-6

LANES = 128
VMEM_LIMIT_BYTES = 56 * 1024 * 1024
FFN_MAIN = 512
FFN_N_MAIN = D_FF // FFN_MAIN
FFN_TAIL = D_FF - FFN_N_MAIN * FFN_MAIN
FFN_TAIL_START = FFN_N_MAIN * FFN_MAIN
assert FFN_TAIL % LANES == 0 and 0 < FFN_TAIL < FFN_MAIN
TM_PROMPT = 512
ATTN_TQ = 256
ATTN_HEADS = 2
HGRN_BLOCK = 16
HGRN_HEADS = 4
HGRN_TS = 512
NEG_BIG = -1e30

_NT = (((1,), (1,)), ((), ()))
_TN = (((0,), (0,)), ((), ()))


def _params(sem):
    return pltpu.CompilerParams(dimension_semantics=sem, vmem_limit_bytes=VMEM_LIMIT_BYTES)


def _resident(shape, index_map):
    return pl.BlockSpec(shape, index_map, pipeline_mode=pl.Buffered(1))


def _layer_norm(z, g, b):
    mu = jnp.mean(z, axis=-1, keepdims=True)
    zc = z - mu
    var = jnp.mean(zc * zc, axis=-1, keepdims=True)
    return zc * lax.rsqrt(var + LN_EPS) * g + b


def _sigmoid(x):
    return 1.0 / (1.0 + jnp.exp(-x))


def _log_sigmoid(x):
    return jnp.minimum(x, 0.0) - jnp.log(1.0 + jnp.exp(-jnp.abs(x)))


def _ffn_kernel(x_ref, wg_ref, wu_ref, wd_ref, wgt_ref, wut_ref, wdt_ref, g_ref, b_ref, y_ref, xb_ref):
    j = pl.program_id(1)

    @pl.when(j == 0)
    def _():
        xb_ref[...] = x_ref[...].astype(BF16)
        y_ref[...] = jnp.zeros_like(y_ref)

    def accumulate(wg, wu, wd):
        xb = xb_ref[...]
        gate = jnp.dot(xb, wg, preferred_element_type=F32)
        up = jnp.dot(xb, wu, preferred_element_type=F32)
        h = (gate * _sigmoid(gate) * up).astype(BF16)
        y_ref[...] += jnp.dot(h, wd, preferred_element_type=F32)

    @pl.when(j < FFN_N_MAIN)
    def _():
        accumulate(wg_ref[...], wu_ref[...], wd_ref[...])

    @pl.when(j >= FFN_N_MAIN)
    def _():
        accumulate(wgt_ref[0], wut_ref[0], wdt_ref[0])

    @pl.when(j == pl.num_programs(1) - 1)
    def _():
        z = ALPHA * x_ref[...] + 0.5 * y_ref[...]
        y_ref[...] = _layer_norm(z, g_ref[...], b_ref[...])


def _ffn(x, wg, wu, wd, layer, g, b, *, tm):
    m = x.shape[0]

    def main(j):
        return jnp.minimum(j, FFN_N_MAIN - 1)

    one, full, tail = pl.Element(1), pl.Element(D_MODEL), pl.Element(FFN_TAIL)
    return pl.pallas_call(
        _ffn_kernel,
        grid=(m // tm, FFN_N_MAIN + 1),
        in_specs=[
            pl.BlockSpec((tm, D_MODEL), lambda i, j: (i, 0)),
            pl.BlockSpec((None, D_MODEL, FFN_MAIN), lambda i, j: (layer, 0, main(j))),
            pl.BlockSpec((None, D_MODEL, FFN_MAIN), lambda i, j: (layer, 0, main(j))),
            pl.BlockSpec((None, FFN_MAIN, D_MODEL), lambda i, j: (layer, main(j), 0)),
            pl.BlockSpec((one, full, tail), lambda i, j: (layer, 0, FFN_TAIL_START)),
            pl.BlockSpec((one, full, tail), lambda i, j: (layer, 0, FFN_TAIL_START)),
            pl.BlockSpec((one, tail, full), lambda i, j: (layer, FFN_TAIL_START, 0)),
            pl.BlockSpec((1, D_MODEL), lambda i, j: (0, 0)),
            pl.BlockSpec((1, D_MODEL), lambda i, j: (0, 0)),
        ],
        out_specs=pl.BlockSpec((tm, D_MODEL), lambda i, j: (i, 0)),
        out_shape=jax.ShapeDtypeStruct((m, D_MODEL), F32),
        scratch_shapes=[pltpu.VMEM((tm, D_MODEL), BF16)],
        compiler_params=_params(("arbitrary", "arbitrary")),
        name="ffn_ln",
    )(x, wg, wu, wd, wg, wu, wd, g, b)


def _proj_kernel(x_ref, w_ref, *refs, kind, n_heads, tm):
    outs, wb_ref = refs[:-1], refs[-1]

    @pl.when(pl.program_id(0) == 0)
    def _():
        wb_ref[...] = w_ref[...].astype(BF16)

    y = jnp.dot(x_ref[...].astype(BF16), wb_ref[...], preferred_element_type=F32)
    if kind == "kv":
        outs[0][...] = y.astype(BF16)
        for h in range(n_heads):
            outs[1][pl.ds(h, tm, stride=n_heads), :] = y[:, h * LANES:(h + 1) * LANES]
    else:
        outs[0][...] = y.astype(outs[0].dtype)


def _proj(x, w, layer, *, col, tn, tm, kind, name):
    m, k = x.shape
    n_heads = tn // LANES
    if kind == "kv":
        out_specs = [pl.BlockSpec((tm, tn), lambda i: (i, 0)),
                     pl.BlockSpec((tm * n_heads, LANES), lambda i: (i, 0))]
        out_shape = [jax.ShapeDtypeStruct((m, tn), BF16), jax.ShapeDtypeStruct((m * n_heads, LANES), F32)]
    else:
        out_specs = pl.BlockSpec((tm, tn), lambda i: (i, 0))
        out_shape = jax.ShapeDtypeStruct((m, tn), BF16 if kind == "bf16" else F32)
    return pl.pallas_call(
        functools.partial(_proj_kernel, kind=kind, n_heads=n_heads, tm=tm),
        grid=(m // tm,),
        in_specs=[
            pl.BlockSpec((tm, k), lambda i: (i, 0)),
            _resident((None, k, tn), lambda i: (layer, 0, col)),
        ],
        out_specs=out_specs,
        out_shape=out_shape,
        scratch_shapes=[pltpu.VMEM((k, tn), BF16)],
        compiler_params=_params(("arbitrary",)),
        name=name,
    )(x, w)


def _logf_kernel(x_ref, w_ref, bf_ref, o_ref):
    fa = jnp.dot(x_ref[...].astype(BF16), w_ref[...], preferred_element_type=F32)
    lane = lax.broadcasted_iota(jnp.int32, fa.shape, 1)
    o_ref[...] = jnp.where(lane < H_A, _log_sigmoid(fa + bf_ref[...]), 0.0)


def _logf(x, w_pad, bf_pad, *, tm):
    m = x.shape[0]
    return pl.pallas_call(
        _logf_kernel,
        grid=(m // tm,),
        in_specs=[
            pl.BlockSpec((tm, D_MODEL), lambda i: (i, 0)),
            pl.BlockSpec((D_MODEL, LANES), lambda i: (0, 0)),
            pl.BlockSpec((1, LANES), lambda i: (0, 0)),
        ],
        out_specs=pl.BlockSpec((tm, LANES), lambda i: (i, 0)),
        out_shape=jax.ShapeDtypeStruct((m, LANES), F32),
        compiler_params=_params(("arbitrary",)),
        name="fox_logf",
    )(x, w_pad, bf_pad)


def _lin_ln_kernel(*refs, n_in):
    a_refs = refs[:n_in]
    w_refs = refs[n_in:2 * n_in]
    x_ref, g_ref, b_ref, y_ref = refs[2 * n_in:2 * n_in + 4]
    wb_refs = refs[2 * n_in + 4:]

    @pl.when(pl.program_id(0) == 0)
    def _():
        for w_ref, wb_ref in zip(w_refs, wb_refs):
            wb_ref[...] = w_ref[...].astype(BF16)

    acc = None
    for a_ref, wb_ref in zip(a_refs, wb_refs):
        part = jnp.dot(a_ref[...], wb_ref[...], preferred_element_type=F32)
        acc = part if acc is None else acc + part
    z = ALPHA * x_ref[...] + acc
    y_ref[...] = _layer_norm(z, g_ref[...], b_ref[...])


def _lin_ln(a_list, w, layer, x, g, b, *, tm, name):
    m = x.shape[0]
    n_in = len(a_list)
    rows = [a.shape[1] for a in a_list]
    assert sum(rows) == w.shape[1] and len(set(rows)) == 1
    in_specs = [pl.BlockSpec((tm, r), lambda i: (i, 0)) for r in rows]
    in_specs += [_resident((None, r, D_MODEL), functools.partial(lambda i, kk: (layer, kk, 0), kk=kk))
                 for kk, r in enumerate(rows)]
    in_specs += [
        pl.BlockSpec((tm, D_MODEL), lambda i: (i, 0)),
        pl.BlockSpec((1, D_MODEL), lambda i: (0, 0)),
        pl.BlockSpec((1, D_MODEL), lambda i: (0, 0)),
    ]
    return pl.pallas_call(
        functools.partial(_lin_ln_kernel, n_in=n_in),
        grid=(m // tm,),
        in_specs=in_specs,
        out_specs=pl.BlockSpec((tm, D_MODEL), lambda i: (i, 0)),
        out_shape=jax.ShapeDtypeStruct((m, D_MODEL), F32),
        scratch_shapes=[pltpu.VMEM((r, D_MODEL), BF16) for r in rows],
        compiler_params=_params(("arbitrary",)),
        name=name,
    )(*a_list, *([w] * n_in), x, g, b)


def _cum_kernel(lf_ref, cc_ref, cr_ref, *, n_blocks):
    r_i = lax.broadcasted_iota(jnp.int32, (LANES, LANES), 0)
    c_i = lax.broadcasted_iota(jnp.int32, (LANES, LANES), 1)
    lower = (r_i >= c_i).astype(F32)
    upper = (r_i <= c_i).astype(F32)
    e_r = lax.broadcasted_iota(jnp.int32, (8, LANES), 0)
    e_c = lax.broadcasted_iota(jnp.int32, (8, LANES), 1)
    pick = (e_r == e_c).astype(F32)

    def body(r, carry):
        carry_row, carry_col = carry
        off = pl.multiple_of(r * LANES, LANES)
        lf = lf_ref[pl.ds(off, LANES), :]
        cc = jnp.dot(lower, lf, preferred_element_type=F32, precision=lax.Precision.HIGHEST) + carry_row
        cc_ref[pl.ds(off, LANES), :] = cc
        lf_row = lax.dot_general(pick, lf, _NT, preferred_element_type=F32, precision=lax.Precision.HIGHEST)
        cr = jnp.dot(lf_row, upper, preferred_element_type=F32, precision=lax.Precision.HIGHEST) + carry_col
        cr_ref[:, pl.ds(off, LANES)] = cr
        return cc[LANES - 1:LANES, :], cr[:, LANES - 1:LANES]

    lax.fori_loop(0, n_blocks, body, (jnp.zeros((1, LANES), F32), jnp.zeros((8, 1), F32)))


def _cum(lf_pad, *, nb, skv):
    return pl.pallas_call(
        functools.partial(_cum_kernel, n_blocks=skv // LANES),
        grid=(nb,),
        in_specs=[pl.BlockSpec((skv, LANES), lambda b: (b, 0))],
        out_specs=[
            pl.BlockSpec((skv, LANES), lambda b: (b, 0)),
            pl.BlockSpec((None, 8, skv), lambda b: (b, 0, 0)),
        ],
        out_shape=[
            jax.ShapeDtypeStruct((nb * skv, LANES), F32),
            jax.ShapeDtypeStruct((nb, 8, skv), F32),
        ],
        compiler_params=_params(("arbitrary",)),
        name="fox_cum",
    )(lf_pad)


def _sweep(lo, hi, body, carry, static):
    if static:
        for j in range(lo, hi):
            carry = body(j, carry)
        return carry
    return lax.fori_loop(lo, hi, body, carry)


def _block_offset(j, tk, static):
    return j * tk if static else pl.multiple_of(j * tk, tk)


def _gather_cache(dst_ref, cache_ref, new_ref, *, head0, heads, n_heads, past, t_new):
    s_all = dst_ref.shape[1]
    for gi in range(heads):
        rows = cache_ref[pl.ds(head0 + gi, past, stride=n_heads), :]
        dst_ref[gi, pl.ds(0, past), :] = rows.astype(BF16)
        dst_ref[gi, pl.ds(past, t_new), :] = new_ref[:, gi * LANES:(gi + 1) * LANES]
        dst_ref[gi, pl.ds(past + t_new, s_all - past - t_new), :] = jnp.zeros((s_all - past - t_new, LANES), BF16)


def _kv_getters(refs, *, tk, heads, cache):
    if cache is None:
        k_ref, v_ref = refs[:2]
        rest = refs[2:]
        return (rest,
                lambda gi, off: k_ref[pl.ds(off, tk), gi * LANES:(gi + 1) * LANES],
                lambda gi, off: v_ref[pl.ds(off, tk), gi * LANES:(gi + 1) * LANES])
    kc_ref, vc_ref, kn_ref, vn_ref = refs[:4]
    ks_ref, vs_ref = refs[-2:]
    head0 = pl.program_id(1) * heads
    _gather_cache(ks_ref, kc_ref, kn_ref, head0=head0, heads=heads, **cache)
    _gather_cache(vs_ref, vc_ref, vn_ref, head0=head0, heads=heads, **cache)
    return (refs[4:-2],
            lambda gi, off: ks_ref[gi, pl.ds(off, tk), :],
            lambda gi, off: vs_ref[gi, pl.ds(off, tk), :])


def _kv_specs(k, v, *, skv, wblk, heads, n_heads, cache):
    if cache is None:
        spec = pl.BlockSpec((skv, wblk), lambda b, h, i: (b, h))
        return [k, v], [spec, spec], []
    cache_k, cache_v, layer, past = cache
    nb = cache_k.shape[1]
    t_new = k.shape[0] // nb
    flat = lambda c: c.reshape(c.shape[0], nb, past * n_heads, LANES)
    c_spec = pl.BlockSpec((None, None, past * n_heads, LANES), lambda b, h, i: (layer, b, 0, 0))
    n_spec = pl.BlockSpec((t_new, wblk), lambda b, h, i: (b, h))
    scratch = [pltpu.VMEM((heads, skv, LANES), BF16), pltpu.VMEM((heads, skv, LANES), BF16)]
    return [flat(cache_k), flat(cache_v), k, v], [c_spec, c_spec, n_spec, n_spec], scratch


def _fox_kernel(q_ref, *refs, q_pos0, tq, tk, heads, single_q, cache):
    (cq_ref, ck_ref, o_ref), k_get, v_get = _kv_getters(refs, tk=tk, heads=heads, cache=cache)
    hg = pl.program_id(1)
    qpos0 = q_pos0 if single_q else q_pos0 + pl.program_id(2) * tq
    n_full = (qpos0 + 1) // tk
    n_all = (qpos0 + tq - 1) // tk + 1
    scale = HEAD_DIM ** -0.5
    lane = lax.broadcasted_iota(jnp.int32, (tq, LANES), 1)
    cq_all = cq_ref[...]
    qs = [q_ref[:, gi * LANES:(gi + 1) * LANES] for gi in range(heads)]
    cqs = [jnp.sum(jnp.where(lane == hg * heads + gi, cq_all, 0.0), axis=-1, keepdims=True) for gi in range(heads)]

    def step(j, carry, masked):
        off = _block_offset(j, tk, single_q)
        out = []
        for gi in range(heads):
            m, l, acc = carry[gi]
            kj = k_get(gi, off)
            vj = v_get(gi, off)
            s = lax.dot_general(qs[gi], kj, _NT, preferred_element_type=F32) * scale
            s = s + (cqs[gi] - ck_ref[gi, :, pl.ds(off, tk)])
            if masked:
                qp = qpos0 + lax.broadcasted_iota(jnp.int32, (tq, tk), 0)
                kp = off + lax.broadcasted_iota(jnp.int32, (tq, tk), 1)
                s = jnp.where(qp >= kp, s, -jnp.inf)
            m_new = jnp.maximum(m, jnp.max(s, axis=-1, keepdims=True))
            a = jnp.exp(m - m_new)
            p = jnp.exp(s - m_new)
            l = a * l + jnp.sum(p, axis=-1, keepdims=True)
            acc = a * acc + jnp.dot(p.astype(BF16), vj, preferred_element_type=F32)
            out.append((m_new, l, acc))
        return tuple(out)

    init = tuple((jnp.full((tq, 1), NEG_BIG, F32), jnp.zeros((tq, 1), F32), jnp.zeros((tq, HEAD_DIM), F32))
                 for _ in range(heads))
    carry = _sweep(0, n_full, functools.partial(step, masked=False), init, single_q)
    carry = _sweep(n_full, n_all, functools.partial(step, masked=True), carry, single_q)
    for gi in range(heads):
        _, l, acc = carry[gi]
        o_ref[:, gi * LANES:(gi + 1) * LANES] = (acc / l).astype(o_ref.dtype)


def _fox(q, k, v, cum_col, cum_row, *, nb, sq, skv, q_pos0, tq, tk, heads, cache=None):
    nq = sq // tq
    cq_blk0 = q_pos0 // tq
    cq_per_b = skv // tq
    wblk = heads * LANES
    kv_ops, kv_specs, scratch = _kv_specs(k, v, skv=skv, wblk=wblk, heads=heads, n_heads=H_A, cache=cache)
    cache_cfg = None if cache is None else dict(n_heads=H_A, past=cache[3], t_new=k.shape[0] // nb)
    return pl.pallas_call(
        functools.partial(_fox_kernel, q_pos0=q_pos0, tq=tq, tk=tk, heads=heads, single_q=nq == 1, cache=cache_cfg),
        grid=(nb, H_A // heads, nq),
        in_specs=[pl.BlockSpec((tq, wblk), lambda b, h, i: (b * nq + i, h))] + kv_specs + [
            pl.BlockSpec((tq, LANES), lambda b, h, i: (b * cq_per_b + cq_blk0 + i, 0)),
            pl.BlockSpec((None, heads, 1, skv), lambda b, h, i: (b, h, 0, 0)),
        ],
        out_specs=pl.BlockSpec((tq, wblk), lambda b, h, i: (b * nq + i, h)),
        out_shape=jax.ShapeDtypeStruct((nb * sq, W_A), BF16),
        scratch_shapes=scratch,
        compiler_params=_params(("arbitrary", "arbitrary", "arbitrary")),
        name="fox_attn",
    )(q, *kv_ops, cum_col, cum_row)


def _sb_kernel(q_ref, *refs, q_pos0, tq, tk, heads, single_q, cache):
    (u_ref, o_ref), k_get, v_get = _kv_getters(refs, tk=tk, heads=heads, cache=cache)
    qpos0 = q_pos0 if single_q else q_pos0 + pl.program_id(2) * tq
    n_full = qpos0 // tk
    n_all = (qpos0 + tq - 2) // tk + 1
    scale = HEAD_DIM ** -0.5
    u = u_ref[...]
    qs = [q_ref[:, gi * LANES:(gi + 1) * LANES] for gi in range(heads)]

    def step(r, carry, masked, j_top):
        j = j_top - r
        off = _block_offset(j, tk, single_q)
        out = []
        for gi in range(heads):
            rem_run, acc = carry[gi]
            kj = k_get(gi, off)
            vj = v_get(gi, off)
            z = lax.dot_general(qs[gi], kj, _NT, preferred_element_type=F32) * scale
            ls = _log_sigmoid(z)
            l1m = ls - z
            if masked:
                qp = qpos0 + lax.broadcasted_iota(jnp.int32, (tq, tk), 0)
                kp = off + lax.broadcasted_iota(jnp.int32, (tq, tk), 1)
                mask = kp < qp
                l1m = jnp.where(mask, l1m, 0.0)
            hi = l1m.astype(BF16)
            lo = (l1m - hi.astype(F32)).astype(BF16)
            rem = (jnp.dot(hi, u, preferred_element_type=F32) + jnp.dot(lo, u, preferred_element_type=F32)
                   + rem_run)
            w = jnp.exp(ls + rem)
            if masked:
                w = jnp.where(mask, w, 0.0)
            acc = acc + jnp.dot(w.astype(BF16), vj, preferred_element_type=F32)
            rem_run = rem_run + jnp.sum(l1m, axis=-1, keepdims=True)
            out.append((rem_run, acc))
        return tuple(out)

    init = tuple((jnp.zeros((tq, 1), F32), jnp.zeros((tq, HEAD_DIM), F32)) for _ in range(heads))
    carry = _sweep(0, n_all - n_full, functools.partial(step, masked=True, j_top=n_all - 1), init, single_q)
    carry = _sweep(0, n_full, functools.partial(step, masked=False, j_top=n_full - 1), carry, single_q)
    for gi in range(heads):
        o_ref[:, gi * LANES:(gi + 1) * LANES] = carry[gi][1].astype(o_ref.dtype)


def _sb(q, k, v, *, nb, sq, skv, q_pos0, tq, tk, heads, cache=None):
    nq = sq // tq
    wblk = heads * LANES
    r = lax.broadcasted_iota(jnp.int32, (tk, tk), 0)
    c = lax.broadcasted_iota(jnp.int32, (tk, tk), 1)
    u = (r > c).astype(BF16)
    kv_ops, kv_specs, scratch = _kv_specs(k, v, skv=skv, wblk=wblk, heads=heads, n_heads=H_C, cache=cache)
    cache_cfg = None if cache is None else dict(n_heads=H_C, past=cache[3], t_new=k.shape[0] // nb)
    return pl.pallas_call(
        functools.partial(_sb_kernel, q_pos0=q_pos0, tq=tq, tk=tk, heads=heads, single_q=nq == 1, cache=cache_cfg),
        grid=(nb, H_C // heads, nq),
        in_specs=[pl.BlockSpec((tq, wblk), lambda b, h, i: (b * nq + i, h))] + kv_specs + [
            pl.BlockSpec((tk, tk), lambda b, h, i: (0, 0)),
        ],
        out_specs=pl.BlockSpec((tq, wblk), lambda b, h, i: (b * nq + i, h)),
        out_shape=jax.ShapeDtypeStruct((nb * sq, W_C), BF16),
        scratch_shapes=scratch,
        compiler_params=_params(("arbitrary", "arbitrary", "arbitrary")),
        name="sb_attn",
    )(q, *kv_ops, u)


def _tile_rows(idx, n_tiles, tile):
    return pl.ds(pl.multiple_of(jnp.minimum(idx, n_tiles - 1) * tile, tile), tile)


def _sb_pipe_kernel(q_ref, k_ref, v_ref, u_ref, o_ref, z_ref, ls_ref, hi_ref, lo_ref, w_ref, *, nq, tq, heads):
    n_pairs = nq * (nq + 1) // 2
    scale = HEAD_DIM ** -0.5
    u = u_ref[...]
    for ref in (z_ref, ls_ref, hi_ref, lo_ref, w_ref):
        ref[...] = jnp.zeros(ref.shape, ref.dtype)
    col_minus_row = (lax.broadcasted_iota(jnp.int32, (tq, tq), 1) - lax.broadcasted_iota(jnp.int32, (tq, tq), 0))

    def rows(idx):
        return _tile_rows(idx, nq, tq)

    def body(t, carry):
        (i1, j1, i2, j2, i3, j3, i4, j4), vec = carry
        new_vec = []
        for gi in range(heads):
            cs = slice(gi * LANES, (gi + 1) * LANES)
            acc, run, tot = vec[gi]
            acc = jnp.where(j4 == i4, 0.0, acc) + jnp.dot(w_ref[gi], v_ref[rows(j4), cs], preferred_element_type=F32)
            o_ref[rows(i4), cs] = acc.astype(o_ref.dtype)
            run = jnp.where(j3 == i3, 0.0, run)
            rem = (jnp.dot(hi_ref[gi], u, preferred_element_type=F32)
                   + jnp.dot(lo_ref[gi], u, preferred_element_type=F32) + run)
            w_ref[gi] = jnp.exp(ls_ref[gi] + rem).astype(BF16)
            run = run + tot
            thr = (jnp.minimum(i2, nq - 1) - jnp.minimum(j2, nq - 1)) * tq
            z = jnp.where(col_minus_row < thr, z_ref[gi], NEG_BIG)
            ls = _log_sigmoid(z)
            l1m = ls - z
            hi = l1m.astype(BF16)
            ls_ref[gi] = ls
            hi_ref[gi] = hi
            lo_ref[gi] = (l1m - hi.astype(F32)).astype(BF16)
            tot = jnp.sum(l1m, axis=-1, keepdims=True)
            z_ref[gi] = lax.dot_general(q_ref[rows(i1), cs], k_ref[rows(j1), cs], _NT,
                                        preferred_element_type=F32) * scale
            new_vec.append((acc, run, tot))
        done = j1 == 0
        nxt = (jnp.where(done, i1 + 1, i1), jnp.where(done, i1 + 1, j1 - 1))
        return (nxt + (i1, j1, i2, j2, i3, j3)), tuple(new_vec)

    zero = jnp.int32(0)
    zero_col = jnp.sum(z_ref[0], axis=-1, keepdims=True)
    vec0 = tuple((jnp.zeros((tq, HEAD_DIM), F32), zero_col, zero_col) for _ in range(heads))
    lax.fori_loop(0, n_pairs + 3, body, ((zero,) * 8, vec0))


def _sb_prompt(q, k, v, *, nb, s, tq, heads):
    wblk = heads * LANES
    r = lax.broadcasted_iota(jnp.int32, (tq, tq), 0)
    c = lax.broadcasted_iota(jnp.int32, (tq, tq), 1)
    u = (r > c).astype(BF16)
    seq = pl.BlockSpec((s, wblk), lambda b, h: (b, h))
    return pl.pallas_call(
        functools.partial(_sb_pipe_kernel, nq=s // tq, tq=tq, heads=heads),
        grid=(nb, H_C // heads),
        in_specs=[seq, seq, seq, pl.BlockSpec((tq, tq), lambda b, h: (0, 0))],
        out_specs=seq,
        out_shape=jax.ShapeDtypeStruct((nb * s, W_C), BF16),
        scratch_shapes=[pltpu.VMEM((heads, tq, tq), F32), pltpu.VMEM((heads, tq, tq), F32),
                        pltpu.VMEM((heads, tq, tq), BF16), pltpu.VMEM((heads, tq, tq), BF16),
                        pltpu.VMEM((heads, tq, tq), BF16)],
        compiler_params=_params(("arbitrary", "arbitrary")),
        name="sb_attn_p",
    )(q, k, v, u)


def _fox_pipe_kernel(q_ref, k_ref, v_ref, cq_ref, ck_ref, o_ref, s_ref, p_ref, *, nq, tq, heads):
    n_pairs = nq * (nq + 1) // 2
    hg = pl.program_id(1)
    scale = HEAD_DIM ** -0.5
    s_ref[...] = jnp.zeros(s_ref.shape, s_ref.dtype)
    p_ref[...] = jnp.zeros(p_ref.shape, p_ref.dtype)
    col_minus_row = (lax.broadcasted_iota(jnp.int32, (tq, tq), 1) - lax.broadcasted_iota(jnp.int32, (tq, tq), 0))
    lane = lax.broadcasted_iota(jnp.int32, (tq, LANES), 1)

    def rows(idx):
        return _tile_rows(idx, nq, tq)

    def body(t, carry):
        (i1, j1, i2, j2, i3, j3), vec = carry
        new_vec = []
        cq_all = cq_ref[rows(i2), :]
        for gi in range(heads):
            cs = slice(gi * LANES, (gi + 1) * LANES)
            acc, m, l, alpha_p, l_p = vec[gi]
            acc = alpha_p * acc + jnp.dot(p_ref[gi], v_ref[rows(j3), cs], preferred_element_type=F32)
            o_ref[rows(i3), cs] = (acc / l_p).astype(o_ref.dtype)
            first = j2 == 0
            m = jnp.where(first, NEG_BIG, m)
            l = jnp.where(first, 0.0, l)
            cq = jnp.sum(jnp.where(lane == hg * heads + gi, cq_all, 0.0), axis=-1, keepdims=True)
            sc = s_ref[gi] + (cq - ck_ref[gi, :, rows(j2)])
            thr = (jnp.minimum(i2, nq - 1) - jnp.minimum(j2, nq - 1)) * tq
            sc = jnp.where(col_minus_row <= thr, sc, -jnp.inf)
            m_new = jnp.maximum(m, jnp.max(sc, axis=-1, keepdims=True))
            alpha_p = jnp.exp(m - m_new)
            p = jnp.exp(sc - m_new)
            l = alpha_p * l + jnp.sum(p, axis=-1, keepdims=True)
            l_p = l
            p_ref[gi] = p.astype(BF16)
            s_ref[gi] = lax.dot_general(q_ref[rows(i1), cs], k_ref[rows(j1), cs], _NT,
                                        preferred_element_type=F32) * scale
            new_vec.append((acc, m_new, l, alpha_p, l_p))
        done = j1 == i1
        nxt = (jnp.where(done, i1 + 1, i1), jnp.where(done, 0, j1 + 1))
        return (nxt + (i1, j1, i2, j2)), tuple(new_vec)

    zero = jnp.int32(0)
    zero_col = jnp.sum(s_ref[0], axis=-1, keepdims=True)
    vec0 = tuple((jnp.zeros((tq, HEAD_DIM), F32), zero_col + NEG_BIG, zero_col, zero_col, zero_col + 1.0)
                 for _ in range(heads))
    lax.fori_loop(0, n_pairs + 2, body, ((zero,) * 6, vec0))


def _fox_prompt(q, k, v, cum_col, cum_row, *, nb, s, tq, heads):
    wblk = heads * LANES
    seq = pl.BlockSpec((s, wblk), lambda b, h: (b, h))
    return pl.pallas_call(
        functools.partial(_fox_pipe_kernel, nq=s // tq, tq=tq, heads=heads),
        grid=(nb, H_A // heads),
        in_specs=[seq, seq, seq,
                  pl.BlockSpec((s, LANES), lambda b, h: (b, 0)),
                  pl.BlockSpec((None, heads, 1, s), lambda b, h: (b, h, 0, 0))],
        out_specs=seq,
        out_shape=jax.ShapeDtypeStruct((nb * s, W_A), BF16),
        scratch_shapes=[pltpu.VMEM((heads, tq, tq), F32), pltpu.VMEM((heads, tq, tq), BF16)],
        compiler_params=_params(("arbitrary", "arbitrary")),
        name="fox_attn_p",
    )(q, k, v, cum_col, cum_row)


def _cross_kernel(q_ref, k_ref, v_ref, o_ref):
    q = q_ref[...]
    k = k_ref[...].astype(BF16)
    v = v_ref[...].astype(BF16)
    s = lax.dot_general(q, k, _NT, preferred_element_type=F32) * (HD_X ** -0.5)
    m = jnp.max(s, axis=-1, keepdims=True)
    p = jnp.exp(s - m)
    l = jnp.sum(p, axis=-1, keepdims=True)
    o = jnp.dot(p.astype(BF16), v, preferred_element_type=F32) / l
    o_ref[...] = o.astype(o_ref.dtype)


def _cross(q, mk, mv, *, nb, sq, tq):
    nq = sq // tq
    return pl.pallas_call(
        _cross_kernel,
        grid=(nb, H_X, nq),
        in_specs=[
            pl.BlockSpec((tq, HD_X), lambda b, h, i: (b * nq + i, h)),
            pl.BlockSpec((N_MEM, HD_X), lambda b, h, i: (b, h)),
            pl.BlockSpec((N_MEM, HD_X), lambda b, h, i: (b, h)),
        ],
        out_specs=pl.BlockSpec((tq, HD_X), lambda b, h, i: (b * nq + i, h)),
        out_shape=jax.ShapeDtypeStruct((nb * sq, D_MODEL), BF16),
        compiler_params=_params(("arbitrary", "arbitrary", "arbitrary")),
        name="cross_attn",
    )(q, mk, mv)


def _hgrn_kernel(q_ref, f_ref, i_ref, g_ref, lb_ref, ng_ref, s0_ref, o_ref, sf_ref, st_ref, *, heads, ts):
    t = pl.program_id(2)
    lbk = HGRN_BLOCK

    @pl.when(t == 0)
    def _():
        for gi in range(heads):
            st_ref[gi] = s0_ref[gi].T

    row = lax.broadcasted_iota(jnp.int32, (lbk, LANES), 0)

    def block(r, carry):
        off = pl.multiple_of(r * lbk, lbk)
        for gi in range(heads):
            cs = slice(gi * LANES, (gi + 1) * LANES)
            xq = q_ref[pl.ds(off, lbk), cs]
            xf = f_ref[pl.ds(off, lbk), cs]
            xi = i_ref[pl.ds(off, lbk), cs]
            xg = g_ref[pl.ds(off, lbk), cs]
            lb = lb_ref[:, cs]
            f = lb + (1.0 - lb) * _sigmoid(xf)
            kk = 1.0 - f
            qq = xq * _sigmoid(xq)
            bcum = jnp.log(f)
            sh = 1
            while sh < lbk:
                bcum = bcum + jnp.where(row >= sh, pltpu.roll(bcum, sh, 0), 0.0)
                sh *= 2
            o = jnp.sum(qq * kk, axis=-1, keepdims=True) * xi
            for d in range(1, lbk):
                e = jnp.exp(bcum - pltpu.roll(bcum, d, 0))
                wgt = jnp.sum(jnp.where(row >= d, qq * pltpu.roll(kk, d, 0) * e, 0.0), axis=-1, keepdims=True)
                o = o + wgt * pltpu.roll(xi, d, 0)
            st = st_ref[gi]
            qe = (qq * jnp.exp(bcum)).astype(BF16)
            o = o + lax.dot_general(qe, st.astype(BF16), _NT, preferred_element_type=F32)
            b_last = bcum[lbk - 1:lbk, :]
            ke = (kk * jnp.exp(b_last - bcum)).astype(BF16)
            st_ref[gi] = st * jnp.exp(b_last) + lax.dot_general(xi.astype(BF16), ke, _TN, preferred_element_type=F32)
            rms = lax.rsqrt(jnp.mean(o * o, axis=-1, keepdims=True) + RMS_EPS)
            gated = o * rms * ng_ref[:, cs] * (xg * _sigmoid(xg))
            o_ref[pl.ds(off, lbk), cs] = gated.astype(o_ref.dtype)
        return carry

    lax.fori_loop(0, ts // lbk, block, 0)

    @pl.when(t == pl.num_programs(2) - 1)
    def _():
        for gi in range(heads):
            sf_ref[gi] = st_ref[gi].T


def _hgrn(hqf, hig, lb, ng, s0, *, nb, s, ts, heads):
    nt = s // ts
    ng_groups = H_B // heads
    wblk = heads * LANES

    def col(seg):
        return lambda b, hg, t: (b * nt + t, seg * ng_groups + hg)

    return pl.pallas_call(
        functools.partial(_hgrn_kernel, heads=heads, ts=ts),
        grid=(nb, ng_groups, nt),
        in_specs=[
            pl.BlockSpec((ts, wblk), col(0)),
            pl.BlockSpec((ts, wblk), col(1)),
            pl.BlockSpec((ts, wblk), col(0)),
            pl.BlockSpec((ts, wblk), col(1)),
            pl.BlockSpec((1, wblk), lambda b, hg, t: (0, hg)),
            pl.BlockSpec((1, wblk), lambda b, hg, t: (0, hg)),
            pl.BlockSpec((None, heads, LANES, LANES), lambda b, hg, t: (b, hg, 0, 0)),
        ],
        out_specs=[
            pl.BlockSpec((ts, wblk), lambda b, hg, t: (b * nt + t, hg)),
            pl.BlockSpec((None, heads, LANES, LANES), lambda b, hg, t: (b, hg, 0, 0)),
        ],
        out_shape=[
            jax.ShapeDtypeStruct((nb * s, W_B), BF16),
            jax.ShapeDtypeStruct((nb, H_B, LANES, LANES), F32),
        ],
        scratch_shapes=[pltpu.VMEM((heads, LANES, LANES), F32)],
        compiler_params=_params(("arbitrary", "arbitrary", "arbitrary")),
        name="hgrn",
    )(hqf, hqf, hig, hig, lb, ng, s0)


def kernel(x_prompt, x_sample, mem_prompt, cache_fox_k, cache_fox_v, cache_fox_logf, state_hgrn, cache_sb_k, cache_sb_v, cache_mem_k, cache_mem_v, ln_g, ln_b, ffn1_w_gate, ffn1_w_up, ffn1_w_down, ffn2_w_gate, ffn2_w_up, ffn2_w_down, x_w_q, x_w_kv, x_w_o, ev_w_in, fox_b_f, hgrn_lb_logits, hgrn_norm_g, ev_w_out, od_w_in, od_w_out):
    nbp, sp, _ = x_prompt.shape
    nbs, ss, _ = x_sample.shape
    past = cache_fox_k.shape[2]
    s_all = ((past + ss + LANES - 1) // LANES) * LANES
    tm_p, tm_s = TM_PROMPT, nbs * ss
    xp = x_prompt.reshape(nbp * sp, D_MODEL)
    xs = x_sample.reshape(nbs * ss, D_MODEL)
    mem = mem_prompt.reshape(nbp * N_MEM, D_MODEL)

    outs = {k: [] for k in ("fk_p", "fv_p", "flf_p", "hs_p", "sk_p", "sv_p", "mk_p", "mv_p",
                            "fk_s", "fv_s", "flf_s", "hs_s", "sk_s", "sv_s")}
    w1 = (ffn1_w_gate.astype(BF16), ffn1_w_up.astype(BF16), ffn1_w_down.astype(BF16))
    w2 = (ffn2_w_gate.astype(BF16), ffn2_w_up.astype(BF16), ffn2_w_down.astype(BF16))
    for l in range(DEPTH):
        g = [ln_g[l, k].reshape(1, D_MODEL) for k in range(4)]
        bb = [ln_b[l, k].reshape(1, D_MODEL) for k in range(4)]
        j = l // 2

        xp = _ffn(xp, *w1, l, g[0], bb[0], tm=tm_p)
        xs = _ffn(xs, *w1, l, g[0], bb[0], tm=tm_s)

        if l % 2 == 0:
            w_in = ev_w_in
            wf = jnp.pad(ev_w_in[j][:, 3 * W_A:3 * W_A + H_A], ((0, 0), (0, LANES - H_A))).astype(BF16)
            w_hb = ev_w_in[:, :, 3 * W_A + H_A:]
            bf_pad = jnp.pad(fox_b_f[j].astype(F32), (0, LANES - H_A)).reshape(1, LANES)
            lb = jnp.cumsum(jax.nn.softmax(hgrn_lb_logits.astype(F32), axis=0), axis=0)[j].reshape(1, W_B)
            ng = hgrn_norm_g[j].astype(F32).reshape(1, W_B)
            w_out = ev_w_out

            def even_proj(x, tm, tag):
                qa = _proj(x, w_in, j, col=0, tn=W_A, tm=tm, kind="bf16", name="ev_q" + tag)
                ka16, ka = _proj(x, w_in, j, col=1, tn=W_A, tm=tm, kind="kv", name="ev_k" + tag)
                va16, va = _proj(x, w_in, j, col=2, tn=W_A, tm=tm, kind="kv", name="ev_v" + tag)
                hqf = _proj(x, w_hb, j, col=0, tn=2 * W_B, tm=tm, kind="f32", name="ev_qf" + tag)
                hig = _proj(x, w_hb, j, col=1, tn=2 * W_B, tm=tm, kind="f32", name="ev_ig" + tag)
                lf = _logf(x, wf, bf_pad, tm=tm)
                return qa, ka16, ka, va16, va, hqf, hig, lf

            qa, ka16, ka, va16, va, hqf, hig, lf = even_proj(xp, tm_p, "")
            cum_col, cum_row = _cum(lf, nb=nbp, skv=sp)
            oa = _fox_prompt(qa, ka16, va16, cum_col, cum_row.reshape(nbp, 8, 1, sp), nb=nbp, s=sp,
                             tq=ATTN_TQ, heads=ATTN_HEADS)
            ob, s_fin = _hgrn(hqf, hig, lb, ng, jnp.zeros((nbp, H_B, LANES, LANES), F32), nb=nbp, s=sp,
                              ts=HGRN_TS, heads=HGRN_HEADS)
            outs["fk_p"].append(ka.reshape(nbp, sp, H_A, HEAD_DIM))
            outs["fv_p"].append(va.reshape(nbp, sp, H_A, HEAD_DIM))
            outs["flf_p"].append(lf[:, :H_A].reshape(nbp, sp, H_A))
            outs["hs_p"].append(s_fin)
            xp = _lin_ln([oa, ob], w_out, j, xp, g[1], bb[1], tm=tm_p, name="ev_out_ln")

            qa, ka16, ka, va16, va, hqf, hig, lf = even_proj(xs, tm_s, "_s")
            c_lf = jnp.pad(cache_fox_logf[j].astype(F32), ((0, 0), (0, 0), (0, LANES - H_A)))
            lf_all = jnp.concatenate(
                [c_lf, lf.reshape(nbs, ss, LANES), jnp.zeros((nbs, s_all - past - ss, LANES), F32)], axis=1)
            cum_col, cum_row = _cum(lf_all.reshape(nbs * s_all, LANES), nb=nbs, skv=s_all)
            oa = _fox(qa, ka16, va16, cum_col, cum_row.reshape(nbs, 8, 1, s_all), nb=nbs, sq=ss, skv=s_all,
                      q_pos0=past, tq=ss, tk=LANES, heads=H_A, cache=(cache_fox_k, cache_fox_v, j, past))
            ob, s_new = _hgrn(hqf, hig, lb, ng, state_hgrn[j].astype(F32), nb=nbs, s=ss, ts=ss, heads=HGRN_HEADS)
            outs["fk_s"].append(ka.reshape(nbs, ss, H_A, HEAD_DIM))
            outs["fv_s"].append(va.reshape(nbs, ss, H_A, HEAD_DIM))
            outs["flf_s"].append(lf[:, :H_A].reshape(nbs, ss, H_A))
            outs["hs_s"].append(s_new)
            xs = _lin_ln([oa, ob], w_out, j, xs, g[1], bb[1], tm=tm_s, name="ev_out_ln_s")
        else:
            w_in = od_w_in
            w_out = od_w_out

            def odd_proj(x, tm, tag):
                q = _proj(x, w_in, j, col=0, tn=W_C, tm=tm, kind="bf16", name="od_q" + tag)
                k16, k = _proj(x, w_in, j, col=1, tn=W_C, tm=tm, kind="kv", name="od_k" + tag)
                v16, v = _proj(x, w_in, j, col=2, tn=W_C, tm=tm, kind="kv", name="od_v" + tag)
                return q, k16, k, v16, v

            q, k16, k, v16, v = odd_proj(xp, tm_p, "")
            o = _sb_prompt(q, k16, v16, nb=nbp, s=sp, tq=ATTN_TQ, heads=ATTN_HEADS)
            outs["sk_p"].append(k.reshape(nbp, sp, H_C, HEAD_DIM))
            outs["sv_p"].append(v.reshape(nbp, sp, H_C, HEAD_DIM))
            xp = _lin_ln([o], w_out, j, xp, g[1], bb[1], tm=tm_p, name="od_out_ln")

            q, k16, k, v16, v = odd_proj(xs, tm_s, "_s")
            o = _sb(q, k16, v16, nb=nbs, sq=ss, skv=s_all, q_pos0=past, tq=ss, tk=LANES, heads=H_C // 2,
                    cache=(cache_sb_k, cache_sb_v, j, past))
            outs["sk_s"].append(k.reshape(nbs, ss, H_C, HEAD_DIM))
            outs["sv_s"].append(v.reshape(nbs, ss, H_C, HEAD_DIM))
            xs = _lin_ln([o], w_out, j, xs, g[1], bb[1], tm=tm_s, name="od_out_ln_s")

        mk = _proj(mem, x_w_kv, l, col=0, tn=D_MODEL, tm=TM_PROMPT, kind="f32", name="mem_k")
        mv = _proj(mem, x_w_kv, l, col=1, tn=D_MODEL, tm=TM_PROMPT, kind="f32", name="mem_v")
        outs["mk_p"].append(mk.reshape(nbp, N_MEM, H_X, HD_X))
        outs["mv_p"].append(mv.reshape(nbp, N_MEM, H_X, HD_X))
        qx = _proj(xp, x_w_q, l, col=0, tn=D_MODEL, tm=tm_p, kind="bf16", name="x_q")
        ox = _cross(qx, mk, mv, nb=nbp, sq=sp, tq=TM_PROMPT)
        xp = _lin_ln([ox], x_w_o, l, xp, g[2], bb[2], tm=tm_p, name="x_out_ln")
        qx = _proj(xs, x_w_q, l, col=0, tn=D_MODEL, tm=tm_s, kind="bf16", name="x_q_s")
        ox = _cross(qx, cache_mem_k[l].reshape(nbs * N_MEM, D_MODEL).astype(F32),
                    cache_mem_v[l].reshape(nbs * N_MEM, D_MODEL).astype(F32), nb=nbs, sq=ss, tq=ss)
        xs = _lin_ln([ox], x_w_o, l, xs, g[2], bb[2], tm=tm_s, name="x_out_ln_s")

        xp = _ffn(xp, *w2, l, g[3], bb[3], tm=tm_p)
        xs = _ffn(xs, *w2, l, g[3], bb[3], tm=tm_s)

    st = {k: jnp.stack(v) for k, v in outs.items()}
    return (xp.reshape(nbp, sp, D_MODEL), xs.reshape(nbs, ss, D_MODEL),
            st["fk_p"], st["fv_p"], st["flf_p"], st["hs_p"], st["sk_p"], st["sv_p"], st["mk_p"], st["mv_p"],
            st["fk_s"], st["fv_s"], st["flf_s"], st["hs_s"], st["sk_s"], st["sv_s"])
```

```python
import functools

import jax
import jax.numpy as jnp
from jax import lax
from jax.experimental import pallas as pl
from jax.experimental.pallas import tpu as pltpu

F32 = jnp.float32
BF16 = jnp.bfloat16

D_MODEL = 2048
DEPTH = 2
HEAD_DIM = 128
H_A = D_MODEL // (2 * HEAD_DIM)
W_A = H_A * HEAD_DIM
H_B = D_MODEL // (2 * HEAD_DIM)
W_B = H_B * 128
H_C = D_MODEL // HEAD_DIM
W_C = H_C * HEAD_DIM
H_X = 4
HD_X = D_MODEL // H_X
N_MEM = 256
D_FF = ((8 * D_MODEL // 3 + 127) // 128) * 128
ALPHA = (2.0 * DEPTH) ** 0.25
LN_EPS = 1e-5
RMS_EPS = 1e-6

LANES = 128
VMEM_LIMIT_BYTES = 56 * 1024 * 1024
FFN_MAIN = 512
FFN_N_MAIN = D_FF // FFN_MAIN
FFN_TAIL = D_FF - FFN_N_MAIN * FFN_MAIN
FFN_TAIL_START = FFN_N_MAIN * FFN_MAIN
assert FFN_TAIL % LANES == 0 and 0 < FFN_TAIL < FFN_MAIN
TM_PROMPT = 512
CROSS_TQ = 1024
SAMPLE_TK = 256
ATTN_TQ = 256
ATTN_HEADS = 2
HGRN_BLOCK = 16
HGRN_HEADS = 4
HGRN_TS = 512
NEG_BIG = -1e30

_NT = (((1,), (1,)), ((), ()))
_TN = (((0,), (0,)), ((), ()))


def _params(sem):
    return pltpu.CompilerParams(dimension_semantics=sem, vmem_limit_bytes=VMEM_LIMIT_BYTES)


def _resident(shape, index_map):
    return pl.BlockSpec(shape, index_map, pipeline_mode=pl.Buffered(1))


def _layer_norm(z, g, b):
    mu = jnp.mean(z, axis=-1, keepdims=True)
    zc = z - mu
    var = jnp.mean(zc * zc, axis=-1, keepdims=True)
    return zc * lax.rsqrt(var + LN_EPS) * g + b


def _sigmoid(x):
    return 1.0 / (1.0 + jnp.exp(-x))


def _log_sigmoid(x):
    return jnp.minimum(x, 0.0) - jnp.log(1.0 + jnp.exp(-jnp.abs(x)))


def _ffn_kernel(x_ref, wg_ref, wu_ref, wd_ref, wgt_ref, wut_ref, wdt_ref, g_ref, b_ref, y_ref):
    j = pl.program_id(1)
    tm = x_ref.shape[0]

    def partial(x, wg, wu, wd):
        xb = x.astype(BF16)
        gate = jnp.dot(xb, wg, preferred_element_type=F32)
        up = jnp.dot(xb, wu, preferred_element_type=F32)
        h = (gate * _sigmoid(gate) * up).astype(BF16)
        return jnp.dot(h, wd, preferred_element_type=F32)

    @pl.when(j == 0)
    def _():
        y_ref[...] = partial(x_ref[...], wg_ref[...], wu_ref[...], wd_ref[...])

    @pl.when(jnp.logical_and(j > 0, j < FFN_N_MAIN))
    def _():
        y_ref[...] += partial(x_ref[...], wg_ref[...], wu_ref[...], wd_ref[...])

    @pl.when(j == FFN_N_MAIN)
    def _():
        half = tm // 2 if tm % 16 == 0 else tm
        for r in range(tm // half):
            rs = slice(r * half, (r + 1) * half)
            x = x_ref[rs, :]
            y = y_ref[rs, :] + partial(x, wgt_ref[0], wut_ref[0], wdt_ref[0])
            y_ref[rs, :] = _layer_norm(ALPHA * x + 0.5 * y, g_ref[...], b_ref[...])


def _ffn(x, wg, wu, wd, layer, g, b, *, tm):
    m = x.shape[0]

    def main(j):
        return jnp.minimum(j, FFN_N_MAIN - 1)

    one, full, tail = pl.Element(1), pl.Element(D_MODEL), pl.Element(FFN_TAIL)
    return pl.pallas_call(
        _ffn_kernel,
        grid=(m // tm, FFN_N_MAIN + 1),
        in_specs=[
            pl.BlockSpec((tm, D_MODEL), lambda i, j: (i, 0)),
            pl.BlockSpec((None, D_MODEL, FFN_MAIN), lambda i, j: (layer, 0, main(j))),
            pl.BlockSpec((None, D_MODEL, FFN_MAIN), lambda i, j: (layer, 0, main(j))),
            pl.BlockSpec((None, FFN_MAIN, D_MODEL), lambda i, j: (layer, main(j), 0)),
            pl.BlockSpec((one, full, tail), lambda i, j: (layer, 0, FFN_TAIL_START)),
            pl.BlockSpec((one, full, tail), lambda i, j: (layer, 0, FFN_TAIL_START)),
            pl.BlockSpec((one, tail, full), lambda i, j: (layer, FFN_TAIL_START, 0)),
            pl.BlockSpec((1, D_MODEL), lambda i, j: (0, 0)),
            pl.BlockSpec((1, D_MODEL), lambda i, j: (0, 0)),
        ],
        out_specs=pl.BlockSpec((tm, D_MODEL), lambda i, j: (i, 0)),
        out_shape=jax.ShapeDtypeStruct((m, D_MODEL), F32),
        compiler_params=_params(("arbitrary", "arbitrary")),
        name="ffn_ln",
    )(x, wg, wu, wd, wg, wu, wd, g, b)


def _proj_kernel(x_ref, w_ref, *refs, kind, n_heads, tm):
    outs, wb_ref = refs[:-1], refs[-1]

    @pl.when(pl.program_id(0) == 0)
    def _():
        wb_ref[...] = w_ref[...].astype(BF16)

    y = jnp.dot(x_ref[...].astype(BF16), wb_ref[...], preferred_element_type=F32)
    if kind == "kv":
        outs[0][...] = y.astype(BF16)
        for h in range(n_heads):
            outs[1][pl.ds(h, tm, stride=n_heads), :] = y[:, h * LANES:(h + 1) * LANES]
    else:
        outs[0][...] = y.astype(outs[0].dtype)


def _proj(x, w, layer, *, col, tn, tm, kind, name):
    m, k = x.shape
    n_heads = tn // LANES
    if kind == "kv":
        out_specs = [pl.BlockSpec((tm, tn), lambda i: (i, 0)),
                     pl.BlockSpec((tm * n_heads, LANES), lambda i: (i, 0))]
        out_shape = [jax.ShapeDtypeStruct((m, tn), BF16), jax.ShapeDtypeStruct((m * n_heads, LANES), F32)]
    else:
        out_specs = pl.BlockSpec((tm, tn), lambda i: (i, 0))
        out_shape = jax.ShapeDtypeStruct((m, tn), BF16 if kind == "bf16" else F32)
    return pl.pallas_call(
        functools.partial(_proj_kernel, kind=kind, n_heads=n_heads, tm=tm),
        grid=(m // tm,),
        in_specs=[
            pl.BlockSpec((tm, k), lambda i: (i, 0)),
            _resident((None, k, tn), lambda i: (layer, 0, col)),
        ],
        out_specs=out_specs,
        out_shape=out_shape,
        scratch_shapes=[pltpu.VMEM((k, tn), BF16)],
        compiler_params=_params(("arbitrary",)),
        name=name,
    )(x, w)


def _logf_kernel(x_ref, w_ref, bf_ref, o_ref):
    fa = jnp.dot(x_ref[...].astype(BF16), w_ref[...], preferred_element_type=F32)
    lane = lax.broadcasted_iota(jnp.int32, fa.shape, 1)
    o_ref[...] = jnp.where(lane < H_A, _log_sigmoid(fa + bf_ref[...]), 0.0)


def _logf(x, w_pad, bf_pad, *, tm):
    m = x.shape[0]
    return pl.pallas_call(
        _logf_kernel,
        grid=(m // tm,),
        in_specs=[
            pl.BlockSpec((tm, D_MODEL), lambda i: (i, 0)),
            pl.BlockSpec((D_MODEL, LANES), lambda i: (0, 0)),
            pl.BlockSpec((1, LANES), lambda i: (0, 0)),
        ],
        out_specs=pl.BlockSpec((tm, LANES), lambda i: (i, 0)),
        out_shape=jax.ShapeDtypeStruct((m, LANES), F32),
        compiler_params=_params(("arbitrary",)),
        name="fox_logf",
    )(x, w_pad, bf_pad)


def _lin_ln_kernel(*refs, n_in):
    a_refs = refs[:n_in]
    w_refs = refs[n_in:2 * n_in]
    x_ref, g_ref, b_ref, y_ref = refs[2 * n_in:2 * n_in + 4]
    wb_refs = refs[2 * n_in + 4:]

    @pl.when(pl.program_id(0) == 0)
    def _():
        for w_ref, wb_ref in zip(w_refs, wb_refs):
            wb_ref[...] = w_ref[...].astype(BF16)

    tm = x_ref.shape[0]
    half = tm // 2 if tm % 32 == 0 else tm
    for r in range(tm // half):
        rs = slice(r * half, (r + 1) * half)
        acc = None
        for a_ref, wb_ref in zip(a_refs, wb_refs):
            part = jnp.dot(a_ref[rs, :], wb_ref[...], preferred_element_type=F32)
            acc = part if acc is None else acc + part
        z = ALPHA * x_ref[rs, :] + acc
        y_ref[rs, :] = _layer_norm(z, g_ref[...], b_ref[...])


def _lin_ln(a_list, w, layer, x, g, b, *, tm, name):
    m = x.shape[0]
    n_in = len(a_list)
    rows = [a.shape[1] for a in a_list]
    assert sum(rows) == w.shape[1] and len(set(rows)) == 1
    in_specs = [pl.BlockSpec((tm, r), lambda i: (i, 0)) for r in rows]
    in_specs += [_resident((None, r, D_MODEL), functools.partial(lambda i, kk: (layer, kk, 0), kk=kk))
                 for kk, r in enumerate(rows)]
    in_specs += [
        pl.BlockSpec((tm, D_MODEL), lambda i: (i, 0)),
        pl.BlockSpec((1, D_MODEL), lambda i: (0, 0)),
        pl.BlockSpec((1, D_MODEL), lambda i: (0, 0)),
    ]
    return pl.pallas_call(
        functools.partial(_lin_ln_kernel, n_in=n_in),
        grid=(m // tm,),
        in_specs=in_specs,
        out_specs=pl.BlockSpec((tm, D_MODEL), lambda i: (i, 0)),
        out_shape=jax.ShapeDtypeStruct((m, D_MODEL), F32),
        scratch_shapes=[pltpu.VMEM((r, D_MODEL), BF16) for r in rows],
        compiler_params=_params(("arbitrary",)),
        name=name,
    )(*a_list, *([w] * n_in), x, g, b)


def _cum_kernel(lf_ref, cc_ref, cr_ref, *, n_blocks):
    r_i = lax.broadcasted_iota(jnp.int32, (LANES, LANES), 0)
    c_i = lax.broadcasted_iota(jnp.int32, (LANES, LANES), 1)
    lower = (r_i >= c_i).astype(F32)
    upper = (r_i <= c_i).astype(F32)
    e_r = lax.broadcasted_iota(jnp.int32, (8, LANES), 0)
    e_c = lax.broadcasted_iota(jnp.int32, (8, LANES), 1)
    pick = (e_r == e_c).astype(F32)

    carry_row = jnp.zeros((1, LANES), F32)
    carry_col = jnp.zeros((8, 1), F32)
    for r in range(n_blocks):
        off = r * LANES
        lf = lf_ref[pl.ds(off, LANES), :]
        cc = jnp.dot(lower, lf, preferred_element_type=F32, precision=lax.Precision.HIGHEST) + carry_row
        cc_ref[pl.ds(off, LANES), :] = cc
        lf_row = lax.dot_general(pick, lf, _NT, preferred_element_type=F32, precision=lax.Precision.HIGHEST)
        cr = jnp.dot(lf_row, upper, preferred_element_type=F32, precision=lax.Precision.HIGHEST) + carry_col
        cr_ref[:, pl.ds(off, LANES)] = cr
        carry_row = cc[LANES - 1:LANES, :]
        carry_col = cr[:, LANES - 1:LANES]


def _cum(lf_pad, *, nb, skv):
    return pl.pallas_call(
        functools.partial(_cum_kernel, n_blocks=skv // LANES),
        grid=(nb,),
        in_specs=[pl.BlockSpec((skv, LANES), lambda b: (b, 0))],
        out_specs=[
            pl.BlockSpec((skv, LANES), lambda b: (b, 0)),
            pl.BlockSpec((None, 8, skv), lambda b: (b, 0, 0)),
        ],
        out_shape=[
            jax.ShapeDtypeStruct((nb * skv, LANES), F32),
            jax.ShapeDtypeStruct((nb, 8, skv), F32),
        ],
        compiler_params=_params(("arbitrary",)),
        name="fox_cum",
    )(lf_pad)


def _sweep(lo, hi, body, carry, static):
    if static:
        for j in range(lo, hi):
            carry = body(j, carry)
        return carry
    return lax.fori_loop(lo, hi, body, carry)


def _block_offset(j, tk, static):
    return j * tk if static else pl.multiple_of(j * tk, tk)


def _gather_cache(dst_ref, cache_ref, new_ref, *, head0, heads, n_heads, past, t_new):
    s_all = dst_ref.shape[1]
    for gi in range(heads):
        rows = cache_ref[pl.ds(head0 + gi, past, stride=n_heads), :]
        dst_ref[gi, pl.ds(0, past), :] = rows.astype(BF16)
        dst_ref[gi, pl.ds(past, t_new), :] = new_ref[:, gi * LANES:(gi + 1) * LANES]
        dst_ref[gi, pl.ds(past + t_new, s_all - past - t_new), :] = jnp.zeros((s_all - past - t_new, LANES), BF16)


def _kv_getters(refs, *, tk, heads, cache):
    if cache is None:
        k_ref, v_ref = refs[:2]
        rest = refs[2:]
        return (rest,
                lambda gi, off: k_ref[pl.ds(off, tk), gi * LANES:(gi + 1) * LANES],
                lambda gi, off: v_ref[pl.ds(off, tk), gi * LANES:(gi + 1) * LANES])
    kc_ref, vc_ref, kn_ref, vn_ref = refs[:4]
    ks_ref, vs_ref = refs[-2:]
    head0 = pl.program_id(1) * heads
    _gather_cache(ks_ref, kc_ref, kn_ref, head0=head0, heads=heads, **cache)
    _gather_cache(vs_ref, vc_ref, vn_ref, head0=head0, heads=heads, **cache)
    return (refs[4:-2],
            lambda gi, off: ks_ref[gi, pl.ds(off, tk), :],
            lambda gi, off: vs_ref[gi, pl.ds(off, tk), :])


def _kv_specs(k, v, *, skv, wblk, heads, n_heads, cache):
    if cache is None:
        spec = pl.BlockSpec((skv, wblk), lambda b, h, i: (b, h))
        return [k, v], [spec, spec], []
    cache_k, cache_v, layer, past = cache
    nb = cache_k.shape[1]
    t_new = k.shape[0] // nb
    flat = lambda c: c.reshape(c.shape[0], nb, past * n_heads, LANES)
    c_spec = pl.BlockSpec((None, None, past * n_heads, LANES), lambda b, h, i: (layer, b, 0, 0))
    n_spec = pl.BlockSpec((t_new, wblk), lambda b, h, i: (b, h))
    scratch = [pltpu.VMEM((heads, skv, LANES), BF16), pltpu.VMEM((heads, skv, LANES), BF16)]
    return [flat(cache_k), flat(cache_v), k, v], [c_spec, c_spec, n_spec, n_spec], scratch


def _fox_kernel(q_ref, *refs, q_pos0, tq, tk, heads, single_q, cache):
    (cq_ref, ck_ref, o_ref), k_get, v_get = _kv_getters(refs, tk=tk, heads=heads, cache=cache)
    hg = pl.program_id(1)
    qpos0 = q_pos0 if single_q else q_pos0 + pl.program_id(2) * tq
    n_full = (qpos0 + 1) // tk
    n_all = (qpos0 + tq - 1) // tk + 1
    scale = HEAD_DIM ** -0.5
    lane = lax.broadcasted_iota(jnp.int32, (tq, LANES), 1)
    cq_all = cq_ref[...]
    qs = [q_ref[:, gi * LANES:(gi + 1) * LANES] for gi in range(heads)]
    cqs = [jnp.sum(jnp.where(lane == hg * heads + gi, cq_all, 0.0), axis=-1, keepdims=True) for gi in range(heads)]

    def step(j, carry, masked):
        off = _block_offset(j, tk, single_q)
        out = []
        for gi in range(heads):
            m, l, acc = carry[gi]
            kj = k_get(gi, off)
            vj = v_get(gi, off)
            s = lax.dot_general(qs[gi], kj, _NT, preferred_element_type=F32) * scale
            s = s + (cqs[gi] - ck_ref[gi, :, pl.ds(off, tk)])
            if masked:
                qp = qpos0 + lax.broadcasted_iota(jnp.int32, (tq, tk), 0)
                kp = off + lax.broadcasted_iota(jnp.int32, (tq, tk), 1)
                s = jnp.where(qp >= kp, s, -jnp.inf)
            m_new = jnp.maximum(m, jnp.max(s, axis=-1, keepdims=True))
            a = jnp.exp(m - m_new)
            p = jnp.exp(s - m_new)
            l = a * l + jnp.sum(p, axis=-1, keepdims=True)
            acc = a * acc + jnp.dot(p.astype(BF16), vj, preferred_element_type=F32)
            out.append((m_new, l, acc))
        return tuple(out)

    init = tuple((jnp.full((tq, 1), NEG_BIG, F32), jnp.zeros((tq, 1), F32), jnp.zeros((tq, HEAD_DIM), F32))
                 for _ in range(heads))
    carry = _sweep(0, n_full, functools.partial(step, masked=False), init, single_q)
    carry = _sweep(n_full, n_all, functools.partial(step, masked=True), carry, single_q)
    for gi in range(heads):
        _, l, acc = carry[gi]
        o_ref[:, gi * LANES:(gi + 1) * LANES] = (acc / l).astype(o_ref.dtype)


def _fox(q, k, v, cum_col, cum_row, *, nb, sq, skv, q_pos0, tq, tk, heads, cache=None):
    nq = sq // tq
    cq_blk0 = q_pos0 // tq
    cq_per_b = skv // tq
    wblk = heads * LANES
    kv_ops, kv_specs, scratch = _kv_specs(k, v, skv=skv, wblk=wblk, heads=heads, n_heads=H_A, cache=cache)
    cache_cfg = None if cache is None else dict(n_heads=H_A, past=cache[3], t_new=k.shape[0] // nb)
    return pl.pallas_call(
        functools.partial(_fox_kernel, q_pos0=q_pos0, tq=tq, tk=tk, heads=heads, single_q=nq == 1, cache=cache_cfg),
        grid=(nb, H_A // heads, nq),
        in_specs=[pl.BlockSpec((tq, wblk), lambda b, h, i: (b * nq + i, h))] + kv_specs + [
            pl.BlockSpec((tq, LANES), lambda b, h, i: (b * cq_per_b + cq_blk0 + i, 0)),
            pl.BlockSpec((None, heads, 1, skv), lambda b, h, i: (b, h, 0, 0)),
        ],
        out_specs=pl.BlockSpec((tq, wblk), lambda b, h, i: (b * nq + i, h)),
        out_shape=jax.ShapeDtypeStruct((nb * sq, W_A), BF16),
        scratch_shapes=scratch,
        compiler_params=_params(("arbitrary", "arbitrary", "arbitrary")),
        name="fox_attn",
    )(q, *kv_ops, cum_col, cum_row)


def _sb_kernel(q_ref, *refs, q_pos0, tq, tk, heads, single_q, cache):
    (u_ref, o_ref), k_get, v_get = _kv_getters(refs, tk=tk, heads=heads, cache=cache)
    qpos0 = q_pos0 if single_q else q_pos0 + pl.program_id(2) * tq
    n_full = qpos0 // tk
    n_all = (qpos0 + tq - 2) // tk + 1
    scale = HEAD_DIM ** -0.5
    u = u_ref[...]
    qs = [q_ref[:, gi * LANES:(gi + 1) * LANES] for gi in range(heads)]

    def step(r, carry, masked, j_top):
        j = j_top - r
        off = _block_offset(j, tk, single_q)
        out = []
        for gi in range(heads):
            rem_run, acc = carry[gi]
            kj = k_get(gi, off)
            vj = v_get(gi, off)
            z = lax.dot_general(qs[gi], kj, _NT, preferred_element_type=F32) * scale
            ls = _log_sigmoid(z)
            l1m = ls - z
            if masked:
                qp = qpos0 + lax.broadcasted_iota(jnp.int32, (tq, tk), 0)
                kp = off + lax.broadcasted_iota(jnp.int32, (tq, tk), 1)
                mask = kp < qp
                l1m = jnp.where(mask, l1m, 0.0)
            hi = l1m.astype(BF16)
            lo = (l1m - hi.astype(F32)).astype(BF16)
            rem = (jnp.dot(hi, u, preferred_element_type=F32) + jnp.dot(lo, u, preferred_element_type=F32)
                   + rem_run)
            w = jnp.exp(ls + rem)
            if masked:
                w = jnp.where(mask, w, 0.0)
            acc = acc + jnp.dot(w.astype(BF16), vj, preferred_element_type=F32)
            rem_run = rem_run + jnp.sum(l1m, axis=-1, keepdims=True)
            out.append((rem_run, acc))
        return tuple(out)

    init = tuple((jnp.zeros((tq, 1), F32), jnp.zeros((tq, HEAD_DIM), F32)) for _ in range(heads))
    carry = _sweep(0, n_all - n_full, functools.partial(step, masked=True, j_top=n_all - 1), init, single_q)
    carry = _sweep(0, n_full, functools.partial(step, masked=False, j_top=n_full - 1), carry, single_q)
    for gi in range(heads):
        o_ref[:, gi * LANES:(gi + 1) * LANES] = carry[gi][1].astype(o_ref.dtype)


def _sb(q, k, v, *, nb, sq, skv, q_pos0, tq, tk, heads, cache=None):
    nq = sq // tq
    wblk = heads * LANES
    r = lax.broadcasted_iota(jnp.int32, (tk, tk), 0)
    c = lax.broadcasted_iota(jnp.int32, (tk, tk), 1)
    u = (r > c).astype(BF16)
    kv_ops, kv_specs, scratch = _kv_specs(k, v, skv=skv, wblk=wblk, heads=heads, n_heads=H_C, cache=cache)
    cache_cfg = None if cache is None else dict(n_heads=H_C, past=cache[3], t_new=k.shape[0] // nb)
    return pl.pallas_call(
        functools.partial(_sb_kernel, q_pos0=q_pos0, tq=tq, tk=tk, heads=heads, single_q=nq == 1, cache=cache_cfg),
        grid=(nb, H_C // heads, nq),
        in_specs=[pl.BlockSpec((tq, wblk), lambda b, h, i: (b * nq + i, h))] + kv_specs + [
            pl.BlockSpec((tk, tk), lambda b, h, i: (0, 0)),
        ],
        out_specs=pl.BlockSpec((tq, wblk), lambda b, h, i: (b * nq + i, h)),
        out_shape=jax.ShapeDtypeStruct((nb * sq, W_C), BF16),
        scratch_shapes=scratch,
        compiler_params=_params(("arbitrary", "arbitrary", "arbitrary")),
        name="sb_attn",
    )(q, *kv_ops, u)


def _tile_rows(idx, n_tiles, tile):
    return pl.ds(pl.multiple_of(jnp.minimum(idx, n_tiles - 1) * tile, tile), tile)


def _sb_pipe_kernel(q_ref, k_ref, v_ref, u_ref, o_ref, z_ref, ls_ref, hi_ref, lo_ref, w_ref, *, nq, tq, heads):
    n_pairs = nq * (nq + 1) // 2
    scale = HEAD_DIM ** -0.5
    u = u_ref[...]
    for ref in (z_ref, ls_ref, hi_ref, lo_ref, w_ref):
        ref[...] = jnp.zeros(ref.shape, ref.dtype)
    col_minus_row = (lax.broadcasted_iota(jnp.int32, (tq, tq), 1) - lax.broadcasted_iota(jnp.int32, (tq, tq), 0))

    def rows(idx):
        return _tile_rows(idx, nq, tq)

    def body(t, carry):
        (i1, j1, i2, j2, i3, j3, i4, j4), vec = carry
        new_vec = []
        for gi in range(heads):
            cs = slice(gi * LANES, (gi + 1) * LANES)
            acc, run, tot = vec[gi]
            acc = jnp.where(j4 == i4, 0.0, acc) + jnp.dot(w_ref[gi], v_ref[rows(j4), cs], preferred_element_type=F32)
            o_ref[rows(i4), cs] = acc.astype(o_ref.dtype)
            run = jnp.where(j3 == i3, 0.0, run)
            rem = (jnp.dot(hi_ref[gi], u, preferred_element_type=F32)
                   + jnp.dot(lo_ref[gi], u, preferred_element_type=F32) + run)
            w_ref[gi] = jnp.exp(ls_ref[gi] + rem).astype(BF16)
            run = run + tot
            thr = (jnp.minimum(i2, nq - 1) - jnp.minimum(j2, nq - 1)) * tq
            z = jnp.where(col_minus_row < thr, z_ref[gi], NEG_BIG)
            ls = _log_sigmoid(z)
            l1m = ls - z
            hi = l1m.astype(BF16)
            ls_ref[gi] = ls
            hi_ref[gi] = hi
            lo_ref[gi] = (l1m - hi.astype(F32)).astype(BF16)
            tot = jnp.sum(l1m, axis=-1, keepdims=True)
            z_ref[gi] = lax.dot_general(q_ref[rows(i1), cs], k_ref[rows(j1), cs], _NT,
                                        preferred_element_type=F32) * scale
            new_vec.append((acc, run, tot))
        done = j1 == 0
        nxt = (jnp.where(done, i1 + 1, i1), jnp.where(done, i1 + 1, j1 - 1))
        return (nxt + (i1, j1, i2, j2, i3, j3)), tuple(new_vec)

    zero = jnp.int32(0)
    zero_col = jnp.sum(z_ref[0], axis=-1, keepdims=True)
    vec0 = tuple((jnp.zeros((tq, HEAD_DIM), F32), zero_col, zero_col) for _ in range(heads))
    lax.fori_loop(0, n_pairs + 3, body, ((zero,) * 8, vec0))


def _sb_prompt(q, k, v, *, nb, s, tq, heads):
    wblk = heads * LANES
    r = lax.broadcasted_iota(jnp.int32, (tq, tq), 0)
    c = lax.broadcasted_iota(jnp.int32, (tq, tq), 1)
    u = (r > c).astype(BF16)
    seq = pl.BlockSpec((s, wblk), lambda b, h: (b, h))
    return pl.pallas_call(
        functools.partial(_sb_pipe_kernel, nq=s // tq, tq=tq, heads=heads),
        grid=(nb, H_C // heads),
        in_specs=[seq, seq, seq, pl.BlockSpec((tq, tq), lambda b, h: (0, 0))],
        out_specs=seq,
        out_shape=jax.ShapeDtypeStruct((nb * s, W_C), BF16),
        scratch_shapes=[pltpu.VMEM((heads, tq, tq), F32), pltpu.VMEM((heads, tq, tq), F32),
                        pltpu.VMEM((heads, tq, tq), BF16), pltpu.VMEM((heads, tq, tq), BF16),
                        pltpu.VMEM((heads, tq, tq), BF16)],
        compiler_params=_params(("arbitrary", "arbitrary")),
        name="sb_attn_p",
    )(q, k, v, u)


def _fox_pipe_kernel(q_ref, k_ref, v_ref, cq_ref, ck_ref, o_ref, s_ref, p_ref, *, nq, tq, heads):
    n_pairs = nq * (nq + 1) // 2
    hg = pl.program_id(1)
    scale = HEAD_DIM ** -0.5
    s_ref[...] = jnp.zeros(s_ref.shape, s_ref.dtype)
    p_ref[...] = jnp.zeros(p_ref.shape, p_ref.dtype)
    col_minus_row = (lax.broadcasted_iota(jnp.int32, (tq, tq), 1) - lax.broadcasted_iota(jnp.int32, (tq, tq), 0))
    lane = lax.broadcasted_iota(jnp.int32, (tq, LANES), 1)

    def rows(idx):
        return _tile_rows(idx, nq, tq)

    def body(t, carry):
        (i1, j1, i2, j2, i3, j3), vec = carry
        new_vec = []
        cq_all = cq_ref[rows(i2), :]
        for gi in range(heads):
            cs = slice(gi * LANES, (gi + 1) * LANES)
            acc, m, l, alpha_p, l_p = vec[gi]
            acc = alpha_p * acc + jnp.dot(p_ref[gi], v_ref[rows(j3), cs], preferred_element_type=F32)
            o_ref[rows(i3), cs] = (acc / l_p).astype(o_ref.dtype)
            first = j2 == 0
            m = jnp.where(first, NEG_BIG, m)
            l = jnp.where(first, 0.0, l)
            cq = jnp.sum(jnp.where(lane == hg * heads + gi, cq_all, 0.0), axis=-1, keepdims=True)
            sc = s_ref[gi] + (cq - ck_ref[gi, :, rows(j2)])
            thr = (jnp.minimum(i2, nq - 1) - jnp.minimum(j2, nq - 1)) * tq
            sc = jnp.where(col_minus_row <= thr, sc, -jnp.inf)
            m_new = jnp.maximum(m, jnp.max(sc, axis=-1, keepdims=True))
            alpha_p = jnp.exp(m - m_new)
            p = jnp.exp(sc - m_new)
            l = alpha_p * l + jnp.sum(p, axis=-1, keepdims=True)
            l_p = l
            p_ref[gi] = p.astype(BF16)
            s_ref[gi] = lax.dot_general(q_ref[rows(i1), cs], k_ref[rows(j1), cs], _NT,
                                        preferred_element_type=F32) * scale
            new_vec.append((acc, m_new, l, alpha_p, l_p))
        done = j1 == i1
        nxt = (jnp.where(done, i1 + 1, i1), jnp.where(done, 0, j1 + 1))
        return (nxt + (i1, j1, i2, j2)), tuple(new_vec)

    zero = jnp.int32(0)
    zero_col = jnp.sum(s_ref[0], axis=-1, keepdims=True)
    vec0 = tuple((jnp.zeros((tq, HEAD_DIM), F32), zero_col + NEG_BIG, zero_col, zero_col, zero_col + 1.0)
                 for _ in range(heads))
    lax.fori_loop(0, n_pairs + 2, body, ((zero,) * 6, vec0))


def _fox_prompt(q, k, v, cum_col, cum_row, *, nb, s, tq, heads):
    wblk = heads * LANES
    seq = pl.BlockSpec((s, wblk), lambda b, h: (b, h))
    return pl.pallas_call(
        functools.partial(_fox_pipe_kernel, nq=s // tq, tq=tq, heads=heads),
        grid=(nb, H_A // heads),
        in_specs=[seq, seq, seq,
                  pl.BlockSpec((s, LANES), lambda b, h: (b, 0)),
                  pl.BlockSpec((None, heads, 1, s), lambda b, h: (b, h, 0, 0))],
        out_specs=seq,
        out_shape=jax.ShapeDtypeStruct((nb * s, W_A), BF16),
        scratch_shapes=[pltpu.VMEM((heads, tq, tq), F32), pltpu.VMEM((heads, tq, tq), BF16)],
        compiler_params=_params(("arbitrary", "arbitrary")),
        name="fox_attn_p",
    )(q, k, v, cum_col, cum_row)


def _cross_kernel(q_ref, k_ref, v_ref, o_ref):
    q = q_ref[...]
    k = k_ref[...].astype(BF16)
    v = v_ref[...].astype(BF16)
    s = lax.dot_general(q, k, _NT, preferred_element_type=F32) * (HD_X ** -0.5)
    m = jnp.max(s, axis=-1, keepdims=True)
    p = jnp.exp(s - m)
    l = jnp.sum(p, axis=-1, keepdims=True)
    o = jnp.dot(p.astype(BF16), v, preferred_element_type=F32) / l
    o_ref[...] = o.astype(o_ref.dtype)


def _cross(q, mk, mv, *, nb, sq, tq):
    nq = sq // tq
    return pl.pallas_call(
        _cross_kernel,
        grid=(nb, H_X, nq),
        in_specs=[
            pl.BlockSpec((tq, HD_X), lambda b, h, i: (b * nq + i, h)),
            pl.BlockSpec((N_MEM, HD_X), lambda b, h, i: (b, h)),
            pl.BlockSpec((N_MEM, HD_X), lambda b, h, i: (b, h)),
        ],
        out_specs=pl.BlockSpec((tq, HD_X), lambda b, h, i: (b * nq + i, h)),
        out_shape=jax.ShapeDtypeStruct((nb * sq, D_MODEL), BF16),
        compiler_params=_params(("arbitrary", "arbitrary", "arbitrary")),
        name="cross_attn",
    )(q, mk, mv)


def _hgrn_kernel(q_ref, f_ref, i_ref, g_ref, lb_ref, ng_ref, s0_ref, o_ref, sf_ref, st_ref, *, heads, ts):
    t = pl.program_id(2)
    lbk = HGRN_BLOCK

    @pl.when(t == 0)
    def _():
        for gi in range(heads):
            st_ref[gi] = s0_ref[gi].T

    row = lax.broadcasted_iota(jnp.int32, (lbk, LANES), 0)

    def block(r, carry):
        off = pl.multiple_of(r * lbk, lbk)
        for gi in range(heads):
            cs = slice(gi * LANES, (gi + 1) * LANES)
            xq = q_ref[pl.ds(off, lbk), cs]
            xf = f_ref[pl.ds(off, lbk), cs]
            xi = i_ref[pl.ds(off, lbk), cs]
            xg = g_ref[pl.ds(off, lbk), cs]
            lb = lb_ref[:, cs]
            f = lb + (1.0 - lb) * _sigmoid(xf)
            kk = 1.0 - f
            qq = xq * _sigmoid(xq)
            bcum = jnp.log(f)
            sh = 1
            while sh < lbk:
                bcum = bcum + jnp.where(row >= sh, pltpu.roll(bcum, sh, 0), 0.0)
                sh *= 2
            o = jnp.sum(qq * kk, axis=-1, keepdims=True) * xi
            ckey = bcum - jnp.log(kk)
            for d in range(1, lbk):
                e = jnp.exp(bcum - pltpu.roll(ckey, d, 0))
                wgt = jnp.sum(jnp.where(row >= d, qq * e, 0.0), axis=-1, keepdims=True)
                o = o + wgt * pltpu.roll(xi, d, 0)
            st = st_ref[gi]
            qe = (qq * jnp.exp(bcum)).astype(BF16)
            o = o + lax.dot_general(qe, st.astype(BF16), _NT, preferred_element_type=F32)
            b_last = bcum[lbk - 1:lbk, :]
            ke = (kk * jnp.exp(b_last - bcum)).astype(BF16)
            st_ref[gi] = st * jnp.exp(b_last) + lax.dot_general(xi.astype(BF16), ke, _TN, preferred_element_type=F32)
            rms = lax.rsqrt(jnp.mean(o * o, axis=-1, keepdims=True) + RMS_EPS)
            gated = o * rms * ng_ref[:, cs] * (xg * _sigmoid(xg))
            o_ref[pl.ds(off, lbk), cs] = gated.astype(o_ref.dtype)
        return carry

    lax.fori_loop(0, ts // lbk, block, 0)

    @pl.when(t == pl.num_programs(2) - 1)
    def _():
        for gi in range(heads):
            sf_ref[gi] = st_ref[gi].T


def _hgrn(hqf, hig, lb, ng, s0, *, nb, s, ts, heads):
    nt = s // ts
    ng_groups = H_B // heads
    wblk = heads * LANES

    def col(seg):
        return lambda b, hg, t: (b * nt + t, seg * ng_groups + hg)

    return pl.pallas_call(
        functools.partial(_hgrn_kernel, heads=heads, ts=ts),
        grid=(nb, ng_groups, nt),
        in_specs=[
            pl.BlockSpec((ts, wblk), col(0)),
            pl.BlockSpec((ts, wblk), col(1)),
            pl.BlockSpec((ts, wblk), col(0)),
            pl.BlockSpec((ts, wblk), col(1)),
            pl.BlockSpec((1, wblk), lambda b, hg, t: (0, hg)),
            pl.BlockSpec((1, wblk), lambda b, hg, t: (0, hg)),
            pl.BlockSpec((None, heads, LANES, LANES), lambda b, hg, t: (b, hg, 0, 0)),
        ],
        out_specs=[
            pl.BlockSpec((ts, wblk), lambda b, hg, t: (b * nt + t, hg)),
            pl.BlockSpec((None, heads, LANES, LANES), lambda b, hg, t: (b, hg, 0, 0)),
        ],
        out_shape=[
            jax.ShapeDtypeStruct((nb * s, W_B), BF16),
            jax.ShapeDtypeStruct((nb, H_B, LANES, LANES), F32),
        ],
        scratch_shapes=[pltpu.VMEM((heads, LANES, LANES), F32)],
        compiler_params=_params(("arbitrary", "arbitrary", "arbitrary")),
        name="hgrn",
    )(hqf, hqf, hig, hig, lb, ng, s0)


def kernel(x_prompt, x_sample, mem_prompt, cache_fox_k, cache_fox_v, cache_fox_logf, state_hgrn, cache_sb_k, cache_sb_v, cache_mem_k, cache_mem_v, ln_g, ln_b, ffn1_w_gate, ffn1_w_up, ffn1_w_down, ffn2_w_gate, ffn2_w_up, ffn2_w_down, x_w_q, x_w_kv, x_w_o, ev_w_in, fox_b_f, hgrn_lb_logits, hgrn_norm_g, ev_w_out, od_w_in, od_w_out):
    nbp, sp, _ = x_prompt.shape
    nbs, ss, _ = x_sample.shape
    past = cache_fox_k.shape[2]
    s_all = ((past + ss + SAMPLE_TK - 1) // SAMPLE_TK) * SAMPLE_TK
    tm_p, tm_s = TM_PROMPT, nbs * ss
    xp = x_prompt.reshape(nbp * sp, D_MODEL)
    xs = x_sample.reshape(nbs * ss, D_MODEL)
    mem = mem_prompt.reshape(nbp * N_MEM, D_MODEL)

    outs = {k: [] for k in ("fk_p", "fv_p", "flf_p", "hs_p", "sk_p", "sv_p", "mk_p", "mv_p",
                            "fk_s", "fv_s", "flf_s", "hs_s", "sk_s", "sv_s")}
    w1 = (ffn1_w_gate.astype(BF16), ffn1_w_up.astype(BF16), ffn1_w_down.astype(BF16))
    w2 = (ffn2_w_gate.astype(BF16), ffn2_w_up.astype(BF16), ffn2_w_down.astype(BF16))
    for l in range(DEPTH):
        g = [ln_g[l, k].reshape(1, D_MODEL) for k in range(4)]
        bb = [ln_b[l, k].reshape(1, D_MODEL) for k in range(4)]
        j = l // 2

        xp = _ffn(xp, *w1, l, g[0], bb[0], tm=tm_p)
        xs = _ffn(xs, *w1, l, g[0], bb[0], tm=tm_s)

        if l % 2 == 0:
            w_in = ev_w_in
            wf = jnp.pad(ev_w_in[j][:, 3 * W_A:3 * W_A + H_A], ((0, 0), (0, LANES - H_A))).astype(BF16)
            w_hb = ev_w_in[:, :, 3 * W_A + H_A:]
            bf_pad = jnp.pad(fox_b_f[j].astype(F32), (0, LANES - H_A)).reshape(1, LANES)
            lb = jnp.cumsum(jax.nn.softmax(hgrn_lb_logits.astype(F32), axis=0), axis=0)[j].reshape(1, W_B)
            ng = hgrn_norm_g[j].astype(F32).reshape(1, W_B)
            w_out = ev_w_out

            def even_proj(x, tm, tag):
                qa = _proj(x, w_in, j, col=0, tn=W_A, tm=tm, kind="bf16", name="ev_q" + tag)
                ka16, ka = _proj(x, w_in, j, col=1, tn=W_A, tm=tm, kind="kv", name="ev_k" + tag)
                va16, va = _proj(x, w_in, j, col=2, tn=W_A, tm=tm, kind="kv", name="ev_v" + tag)
                hqf = _proj(x, w_hb, j, col=0, tn=2 * W_B, tm=tm, kind="f32", name="ev_qf" + tag)
                hig = _proj(x, w_hb, j, col=1, tn=2 * W_B, tm=tm, kind="f32", name="ev_ig" + tag)
                lf = _logf(x, wf, bf_pad, tm=tm)
                return qa, ka16, ka, va16, va, hqf, hig, lf

            qa, ka16, ka, va16, va, hqf, hig, lf = even_proj(xp, tm_p, "")
            cum_col, cum_row = _cum(lf, nb=nbp, skv=sp)
            oa = _fox_prompt(qa, ka16, va16, cum_col, cum_row.reshape(nbp, 8, 1, sp), nb=nbp, s=sp,
                             tq=ATTN_TQ, heads=ATTN_HEADS)
            ob, s_fin = _hgrn(hqf, hig, lb, ng, jnp.zeros((nbp, H_B, LANES, LANES), F32), nb=nbp, s=sp,
                              ts=HGRN_TS, heads=HGRN_HEADS)
            outs["fk_p"].append(ka.reshape(nbp, sp, H_A, HEAD_DIM))
            outs["fv_p"].append(va.reshape(nbp, sp, H_A, HEAD_DIM))
            outs["flf_p"].append(lf[:, :H_A].reshape(nbp, sp, H_A))
            outs["hs_p"].append(s_fin)
            xp = _lin_ln([oa, ob], w_out, j, xp, g[1], bb[1], tm=tm_p, name="ev_out_ln")

            qa, ka16, ka, va16, va, hqf, hig, lf = even_proj(xs, tm_s, "_s")
            c_lf = jnp.pad(cache_fox_logf[j].astype(F32), ((0, 0), (0, 0), (0, LANES - H_A)))
            lf_all = jnp.concatenate(
                [c_lf, lf.reshape(nbs, ss, LANES), jnp.zeros((nbs, s_all - past - ss, LANES), F32)], axis=1)
            cum_col, cum_row = _cum(lf_all.reshape(nbs * s_all, LANES), nb=nbs, skv=s_all)
            oa = _fox(qa, ka16, va16, cum_col, cum_row.reshape(nbs, 8, 1, s_all), nb=nbs, sq=ss, skv=s_all,
                      q_pos0=past, tq=ss, tk=SAMPLE_TK, heads=H_A, cache=(cache_fox_k, cache_fox_v, j, past))
            ob, s_new = _hgrn(hqf, hig, lb, ng, state_hgrn[j].astype(F32), nb=nbs, s=ss, ts=ss, heads=HGRN_HEADS)
            outs["fk_s"].append(ka.reshape(nbs, ss, H_A, HEAD_DIM))
            outs["fv_s"].append(va.reshape(nbs, ss, H_A, HEAD_DIM))
            outs["flf_s"].append(lf[:, :H_A].reshape(nbs, ss, H_A))
            outs["hs_s"].append(s_new)
            xs = _lin_ln([oa, ob], w_out, j, xs, g[1], bb[1], tm=tm_s, name="ev_out_ln_s")
        else:
            w_in = od_w_in
            w_out = od_w_out

            def odd_proj(x, tm, tag):
                q = _proj(x, w_in, j, col=0, tn=W_C, tm=tm, kind="bf16", name="od_q" + tag)
                k16, k = _proj(x, w_in, j, col=1, tn=W_C, tm=tm, kind="kv", name="od_k" + tag)
                v16, v = _proj(x, w_in, j, col=2, tn=W_C, tm=tm, kind="kv", name="od_v" + tag)
                return q, k16, k, v16, v

            q, k16, k, v16, v = odd_proj(xp, tm_p, "")
            o = _sb_prompt(q, k16, v16, nb=nbp, s=sp, tq=ATTN_TQ, heads=ATTN_HEADS)
            outs["sk_p"].append(k.reshape(nbp, sp, H_C, HEAD_DIM))
            outs["sv_p"].append(v.reshape(nbp, sp, H_C, HEAD_DIM))
            xp = _lin_ln([o], w_out, j, xp, g[1], bb[1], tm=tm_p, name="od_out_ln")

            q, k16, k, v16, v = odd_proj(xs, tm_s, "_s")
            o = _sb(q, k16, v16, nb=nbs, sq=ss, skv=s_all, q_pos0=past, tq=ss, tk=SAMPLE_TK, heads=H_C // 2,
                    cache=(cache_sb_k, cache_sb_v, j, past))
            outs["sk_s"].append(k.reshape(nbs, ss, H_C, HEAD_DIM))
            outs["sv_s"].append(v.reshape(nbs, ss, H_C, HEAD_DIM))
            xs = _lin_ln([o], w_out, j, xs, g[1], bb[1], tm=tm_s, name="od_out_ln_s")

        mk = _proj(mem, x_w_kv, l, col=0, tn=D_MODEL, tm=TM_PROMPT, kind="f32", name="mem_k")
        mv = _proj(mem, x_w_kv, l, col=1, tn=D_MODEL, tm=TM_PROMPT, kind="f32", name="mem_v")
        outs["mk_p"].append(mk.reshape(nbp, N_MEM, H_X, HD_X))
        outs["mv_p"].append(mv.reshape(nbp, N_MEM, H_X, HD_X))
        qx = _proj(xp, x_w_q, l, col=0, tn=D_MODEL, tm=tm_p, kind="bf16", name="x_q")
        ox = _cross(qx, mk, mv, nb=nbp, sq=sp, tq=CROSS_TQ)
        xp = _lin_ln([ox], x_w_o, l, xp, g[2], bb[2], tm=tm_p, name="x_out_ln")
        qx = _proj(xs, x_w_q, l, col=0, tn=D_MODEL, tm=tm_s, kind="bf16", name="x_q_s")
        ox = _cross(qx, cache_mem_k[l].reshape(nbs * N_MEM, D_MODEL).astype(F32),
                    cache_mem_v[l].reshape(nbs * N_MEM, D_MODEL).astype(F32), nb=nbs, sq=ss, tq=ss)
        xs = _lin_ln([ox], x_w_o, l, xs, g[2], bb[2], tm=tm_s, name="x_out_ln_s")

        xp = _ffn(xp, *w2, l, g[3], bb[3], tm=tm_p)
        xs = _ffn(xs, *w2, l, g[3], bb[3], tm=tm_s)

    st = {k: jnp.stack(v) for k, v in outs.items()}
    return (xp.reshape(nbp, sp, D_MODEL), xs.reshape(nbs, ss, D_MODEL),
            st["fk_p"], st["fv_p"], st["flf_p"], st["hs_p"], st["sk_p"], st["sv_p"], st["mk_p"], st["mv_p"],
            st["fk_s"], st["fv_s"], st["flf_s"], st["hs_s"], st["sk_s"], st["sv_s"])
```

```python
import functools

import jax
import jax.numpy as jnp
from jax import lax
from jax.experimental import pallas as pl
from jax.experimental.pallas import tpu as pltpu

F32 = jnp.float32
BF16 = jnp.bfloat16

D_MODEL = 2048
DEPTH = 2
HEAD_DIM = 128
H_A = D_MODEL // (2 * HEAD_DIM)
W_A = H_A * HEAD_DIM
H_B = D_MODEL // (2 * HEAD_DIM)
W_B = H_B * 128
H_C = D_MODEL // HEAD_DIM
W_C = H_C * HEAD_DIM
H_X = 4
HD_X = D_MODEL // H_X
N_MEM = 256
D_FF = ((8 * D_MODEL // 3 + 127) // 128) * 128
ALPHA = (2.0 * DEPTH) ** 0.25
LN_EPS = 1e-5
RMS_EPS = 1e-6

LANES = 128
VMEM_LIMIT_BYTES = 56 * 1024 * 1024
FFN_MAIN = 1024
FFN_N_MAIN = D_FF // FFN_MAIN
FFN_TAIL = D_FF - FFN_N_MAIN * FFN_MAIN
FFN_TAIL_START = FFN_N_MAIN * FFN_MAIN
assert FFN_TAIL % LANES == 0 and 0 < FFN_TAIL < FFN_MAIN
TM_PROMPT = 512
CROSS_TQ = 1024
SAMPLE_TK = 256
ATTN_TQ = 256
ATTN_HEADS = 2
HGRN_BLOCK = 16
HGRN_HEADS = 4
HGRN_TS = 512
NEG_BIG = -1e30

_NT = (((1,), (1,)), ((), ()))
_TN = (((0,), (0,)), ((), ()))


def _params(sem):
    return pltpu.CompilerParams(dimension_semantics=sem, vmem_limit_bytes=VMEM_LIMIT_BYTES)


def _resident(shape, index_map):
    return pl.BlockSpec(shape, index_map, pipeline_mode=pl.Buffered(1))


def _layer_norm(z, g, b):
    mu = jnp.mean(z, axis=-1, keepdims=True)
    zc = z - mu
    var = jnp.mean(zc * zc, axis=-1, keepdims=True)
    return zc * lax.rsqrt(var + LN_EPS) * g + b


def _sigmoid(x):
    return 1.0 / (1.0 + jnp.exp(-x))


def _log_sigmoid(x):
    return jnp.minimum(x, 0.0) - jnp.log(1.0 + jnp.exp(-jnp.abs(x)))


def _ffn_kernel(x_ref, wg_ref, wu_ref, wd_ref, wgt_ref, wut_ref, wdt_ref, g_ref, b_ref, y_ref):
    j = pl.program_id(1)
    tm = x_ref.shape[0]

    def partial(x, wg, wu, wd):
        xb = x.astype(BF16)
        gate = jnp.dot(xb, wg, preferred_element_type=F32)
        up = jnp.dot(xb, wu, preferred_element_type=F32)
        h = (gate * _sigmoid(gate) * up).astype(BF16)
        return jnp.dot(h, wd, preferred_element_type=F32)

    @pl.when(j == 0)
    def _():
        y_ref[...] = partial(x_ref[...], wg_ref[...], wu_ref[...], wd_ref[...])

    @pl.when(jnp.logical_and(j > 0, j < FFN_N_MAIN))
    def _():
        y_ref[...] += partial(x_ref[...], wg_ref[...], wu_ref[...], wd_ref[...])

    @pl.when(j == FFN_N_MAIN)
    def _():
        half = tm // 2 if tm % 16 == 0 else tm
        for r in range(tm // half):
            rs = slice(r * half, (r + 1) * half)
            x = x_ref[rs, :]
            y = y_ref[rs, :] + partial(x, wgt_ref[0], wut_ref[0], wdt_ref[0])
            y_ref[rs, :] = _layer_norm(ALPHA * x + 0.5 * y, g_ref[...], b_ref[...])


def _ffn(x, wg, wu, wd, layer, g, b, *, tm):
    m = x.shape[0]

    def main(j):
        return jnp.minimum(j, FFN_N_MAIN - 1)

    one, full, tail = pl.Element(1), pl.Element(D_MODEL), pl.Element(FFN_TAIL)
    return pl.pallas_call(
        _ffn_kernel,
        grid=(m // tm, FFN_N_MAIN + 1),
        in_specs=[
            pl.BlockSpec((tm, D_MODEL), lambda i, j: (i, 0)),
            pl.BlockSpec((None, D_MODEL, FFN_MAIN), lambda i, j: (layer, 0, main(j))),
            pl.BlockSpec((None, D_MODEL, FFN_MAIN), lambda i, j: (layer, 0, main(j))),
            pl.BlockSpec((None, FFN_MAIN, D_MODEL), lambda i, j: (layer, main(j), 0)),
            _resident((one, full, tail), lambda i, j: (layer, 0, FFN_TAIL_START)),
            _resident((one, full, tail), lambda i, j: (layer, 0, FFN_TAIL_START)),
            _resident((one, tail, full), lambda i, j: (layer, FFN_TAIL_START, 0)),
            pl.BlockSpec((1, D_MODEL), lambda i, j: (0, 0)),
            pl.BlockSpec((1, D_MODEL), lambda i, j: (0, 0)),
        ],
        out_specs=pl.BlockSpec((tm, D_MODEL), lambda i, j: (i, 0)),
        out_shape=jax.ShapeDtypeStruct((m, D_MODEL), F32),
        compiler_params=_params(("arbitrary", "arbitrary")),
        name="ffn_ln",
    )(x, wg, wu, wd, wg, wu, wd, g, b)


def _proj_kernel(x_ref, w_ref, *refs, kind, n_heads, tm):
    outs, wb_ref = refs[:-1], refs[-1]

    @pl.when(pl.program_id(0) == 0)
    def _():
        wb_ref[...] = w_ref[...].astype(BF16)

    y = jnp.dot(x_ref[...].astype(BF16), wb_ref[...], preferred_element_type=F32)
    if kind == "kv":
        outs[0][...] = y.astype(BF16)
        for h in range(n_heads):
            outs[1][pl.ds(h, tm, stride=n_heads), :] = y[:, h * LANES:(h + 1) * LANES]
    else:
        outs[0][...] = y.astype(outs[0].dtype)


def _proj(x, w, layer, *, col, tn, tm, kind, name):
    m, k = x.shape
    n_heads = tn // LANES
    if kind == "kv":
        out_specs = [pl.BlockSpec((tm, tn), lambda i: (i, 0)),
                     pl.BlockSpec((tm * n_heads, LANES), lambda i: (i, 0))]
        out_shape = [jax.ShapeDtypeStruct((m, tn), BF16), jax.ShapeDtypeStruct((m * n_heads, LANES), F32)]
    else:
        out_specs = pl.BlockSpec((tm, tn), lambda i: (i, 0))
        out_shape = jax.ShapeDtypeStruct((m, tn), BF16 if kind == "bf16" else F32)
    return pl.pallas_call(
        functools.partial(_proj_kernel, kind=kind, n_heads=n_heads, tm=tm),
        grid=(m // tm,),
        in_specs=[
            pl.BlockSpec((tm, k), lambda i: (i, 0)),
            _resident((None, k, tn), lambda i: (layer, 0, col)),
        ],
        out_specs=out_specs,
        out_shape=out_shape,
        scratch_shapes=[pltpu.VMEM((k, tn), BF16)],
        compiler_params=_params(("arbitrary",)),
        name=name,
    )(x, w)


def _logf_kernel(x_ref, w_ref, bf_ref, o_ref):
    fa = jnp.dot(x_ref[...].astype(BF16), w_ref[...].astype(BF16), preferred_element_type=F32)
    lane = lax.broadcasted_iota(jnp.int32, fa.shape, 1)
    o_ref[...] = jnp.where(lane < H_A, _log_sigmoid(fa + bf_ref[...]), 0.0)


def _logf(x, w, layer, bf_pad, *, tm):
    m = x.shape[0]
    assert (3 * W_A) % LANES == 0
    return pl.pallas_call(
        _logf_kernel,
        grid=(m // tm,),
        in_specs=[
            pl.BlockSpec((tm, D_MODEL), lambda i: (i, 0)),
            pl.BlockSpec((None, D_MODEL, LANES), lambda i: (layer, 0, 3 * W_A // LANES)),
            pl.BlockSpec((1, LANES), lambda i: (0, 0)),
        ],
        out_specs=pl.BlockSpec((tm, LANES), lambda i: (i, 0)),
        out_shape=jax.ShapeDtypeStruct((m, LANES), F32),
        compiler_params=_params(("arbitrary",)),
        name="fox_logf",
    )(x, w, bf_pad)


def _lin_ln_kernel(*refs, n_in):
    a_refs = refs[:n_in]
    w_refs = refs[n_in:2 * n_in]
    x_ref, g_ref, b_ref, y_ref = refs[2 * n_in:2 * n_in + 4]
    wb_refs = refs[2 * n_in + 4:]

    @pl.when(pl.program_id(0) == 0)
    def _():
        for w_ref, wb_ref in zip(w_refs, wb_refs):
            wb_ref[...] = w_ref[...].astype(BF16)

    tm = x_ref.shape[0]
    half = tm // 2 if tm % 32 == 0 else tm
    for r in range(tm // half):
        rs = slice(r * half, (r + 1) * half)
        acc = None
        for a_ref, wb_ref in zip(a_refs, wb_refs):
            part = jnp.dot(a_ref[rs, :], wb_ref[...], preferred_element_type=F32)
            acc = part if acc is None else acc + part
        z = ALPHA * x_ref[rs, :] + acc
        y_ref[rs, :] = _layer_norm(z, g_ref[...], b_ref[...])


def _lin_ln(a_list, w, layer, x, g, b, *, tm, name):
    m = x.shape[0]
    n_in = len(a_list)
    rows = [a.shape[1] for a in a_list]
    assert sum(rows) == w.shape[1] and len(set(rows)) == 1
    in_specs = [pl.BlockSpec((tm, r), lambda i: (i, 0)) for r in rows]
    in_specs += [_resident((None, r, D_MODEL), functools.partial(lambda i, kk: (layer, kk, 0), kk=kk))
                 for kk, r in enumerate(rows)]
    in_specs += [
        pl.BlockSpec((tm, D_MODEL), lambda i: (i, 0)),
        pl.BlockSpec((1, D_MODEL), lambda i: (0, 0)),
        pl.BlockSpec((1, D_MODEL), lambda i: (0, 0)),
    ]
    return pl.pallas_call(
        functools.partial(_lin_ln_kernel, n_in=n_in),
        grid=(m // tm,),
        in_specs=in_specs,
        out_specs=pl.BlockSpec((tm, D_MODEL), lambda i: (i, 0)),
        out_shape=jax.ShapeDtypeStruct((m, D_MODEL), F32),
        scratch_shapes=[pltpu.VMEM((r, D_MODEL), BF16) for r in rows],
        compiler_params=_params(("arbitrary",)),
        name=name,
    )(*a_list, *([w] * n_in), x, g, b)


def _cum_kernel(lf_ref, cc_ref, cr_ref, *, n_blocks):
    r_i = lax.broadcasted_iota(jnp.int32, (LANES, LANES), 0)
    c_i = lax.broadcasted_iota(jnp.int32, (LANES, LANES), 1)
    lower = (r_i >= c_i).astype(F32)
    upper = (r_i <= c_i).astype(F32)
    e_r = lax.broadcasted_iota(jnp.int32, (8, LANES), 0)
    e_c = lax.broadcasted_iota(jnp.int32, (8, LANES), 1)
    pick = (e_r == e_c).astype(F32)

    carry_row = jnp.zeros((1, LANES), F32)
    carry_col = jnp.zeros((8, 1), F32)
    for r in range(n_blocks):
        off = r * LANES
        lf = lf_ref[pl.ds(off, LANES), :]
        cc = jnp.dot(lower, lf, preferred_element_type=F32, precision=lax.Precision.HIGHEST) + carry_row
        cc_ref[pl.ds(off, LANES), :] = cc
        lf_row = lax.dot_general(pick, lf, _NT, preferred_element_type=F32, precision=lax.Precision.HIGHEST)
        cr = jnp.dot(lf_row, upper, preferred_element_type=F32, precision=lax.Precision.HIGHEST) + carry_col
        cr_ref[:, pl.ds(off, LANES)] = cr
        carry_row = cc[LANES - 1:LANES, :]
        carry_col = cr[:, LANES - 1:LANES]


def _cum(lf_pad, *, nb, skv):
    return pl.pallas_call(
        functools.partial(_cum_kernel, n_blocks=skv // LANES),
        grid=(nb,),
        in_specs=[pl.BlockSpec((skv, LANES), lambda b: (b, 0))],
        out_specs=[
            pl.BlockSpec((skv, LANES), lambda b: (b, 0)),
            pl.BlockSpec((None, 8, skv), lambda b: (b, 0, 0)),
        ],
        out_shape=[
            jax.ShapeDtypeStruct((nb * skv, LANES), F32),
            jax.ShapeDtypeStruct((nb, 8, skv), F32),
        ],
        compiler_params=_params(("arbitrary",)),
        name="fox_cum",
    )(lf_pad)


def _sweep(lo, hi, body, carry, static):
    if static:
        for j in range(lo, hi):
            carry = body(j, carry)
        return carry
    return lax.fori_loop(lo, hi, body, carry)


def _block_offset(j, tk, static):
    return j * tk if static else pl.multiple_of(j * tk, tk)


def _gather_cache(dst_ref, cache_ref, new_ref, *, head0, heads, n_heads, past, t_new):
    s_all = dst_ref.shape[1]
    for gi in range(heads):
        rows = cache_ref[pl.ds(head0 + gi, past, stride=n_heads), :]
        dst_ref[gi, pl.ds(0, past), :] = rows.astype(BF16)
        dst_ref[gi, pl.ds(past, t_new), :] = new_ref[:, gi * LANES:(gi + 1) * LANES]
        dst_ref[gi, pl.ds(past + t_new, s_all - past - t_new), :] = jnp.zeros((s_all - past - t_new, LANES), BF16)


def _kv_getters(refs, *, tk, heads, cache):
    if cache is None:
        k_ref, v_ref = refs[:2]
        rest = refs[2:]
        return (rest,
                lambda gi, off: k_ref[pl.ds(off, tk), gi * LANES:(gi + 1) * LANES],
                lambda gi, off: v_ref[pl.ds(off, tk), gi * LANES:(gi + 1) * LANES])
    kc_ref, vc_ref, kn_ref, vn_ref = refs[:4]
    ks_ref, vs_ref = refs[-2:]
    head0 = pl.program_id(1) * heads
    _gather_cache(ks_ref, kc_ref, kn_ref, head0=head0, heads=heads, **cache)
    _gather_cache(vs_ref, vc_ref, vn_ref, head0=head0, heads=heads, **cache)
    return (refs[4:-2],
            lambda gi, off: ks_ref[gi, pl.ds(off, tk), :],
            lambda gi, off: vs_ref[gi, pl.ds(off, tk), :])


def _kv_specs(k, v, *, skv, wblk, heads, n_heads, cache):
    if cache is None:
        spec = pl.BlockSpec((skv, wblk), lambda b, h, i: (b, h))
        return [k, v], [spec, spec], []
    cache_k, cache_v, layer, past = cache
    nb = cache_k.shape[1]
    t_new = k.shape[0] // nb
    flat = lambda c: c.reshape(c.shape[0], nb, past * n_heads, LANES)
    c_spec = pl.BlockSpec((None, None, past * n_heads, LANES), lambda b, h, i: (layer, b, 0, 0))
    n_spec = pl.BlockSpec((t_new, wblk), lambda b, h, i: (b, h))
    scratch = [pltpu.VMEM((heads, skv, LANES), BF16), pltpu.VMEM((heads, skv, LANES), BF16)]
    return [flat(cache_k), flat(cache_v), k, v], [c_spec, c_spec, n_spec, n_spec], scratch


def _fox_kernel(q_ref, *refs, q_pos0, tq, tk, heads, single_q, cache):
    (cq_ref, ck_ref, o_ref), k_get, v_get = _kv_getters(refs, tk=tk, heads=heads, cache=cache)
    hg = pl.program_id(1)
    qpos0 = q_pos0 if single_q else q_pos0 + pl.program_id(2) * tq
    n_full = (qpos0 + 1) // tk
    n_all = (qpos0 + tq - 1) // tk + 1
    scale = HEAD_DIM ** -0.5
    lane = lax.broadcasted_iota(jnp.int32, (tq, LANES), 1)
    cq_all = cq_ref[...]
    qs = [q_ref[:, gi * LANES:(gi + 1) * LANES] for gi in range(heads)]
    cqs = [jnp.sum(jnp.where(lane == hg * heads + gi, cq_all, 0.0), axis=-1, keepdims=True) for gi in range(heads)]

    def step(j, carry, masked):
        off = _block_offset(j, tk, single_q)
        out = []
        for gi in range(heads):
            m, l, acc = carry[gi]
            kj = k_get(gi, off)
            vj = v_get(gi, off)
            s = lax.dot_general(qs[gi], kj, _NT, preferred_element_type=F32) * scale
            s = s + (cqs[gi] - ck_ref[gi, :, pl.ds(off, tk)])
            if masked:
                qp = qpos0 + lax.broadcasted_iota(jnp.int32, (tq, tk), 0)
                kp = off + lax.broadcasted_iota(jnp.int32, (tq, tk), 1)
                s = jnp.where(qp >= kp, s, -jnp.inf)
            m_new = jnp.maximum(m, jnp.max(s, axis=-1, keepdims=True))
            a = jnp.exp(m - m_new)
            p = jnp.exp(s - m_new)
            l = a * l + jnp.sum(p, axis=-1, keepdims=True)
            acc = a * acc + jnp.dot(p.astype(BF16), vj, preferred_element_type=F32)
            out.append((m_new, l, acc))
        return tuple(out)

    init = tuple((jnp.full((tq, 1), NEG_BIG, F32), jnp.zeros((tq, 1), F32), jnp.zeros((tq, HEAD_DIM), F32))
                 for _ in range(heads))
    carry = _sweep(0, n_full, functools.partial(step, masked=False), init, single_q)
    carry = _sweep(n_full, n_all, functools.partial(step, masked=True), carry, single_q)
    for gi in range(heads):
        _, l, acc = carry[gi]
        o_ref[:, gi * LANES:(gi + 1) * LANES] = (acc / l).astype(o_ref.dtype)


def _fox(q, k, v, cum_col, cum_row, *, nb, sq, skv, q_pos0, tq, tk, heads, cache=None):
    nq = sq // tq
    cq_blk0 = q_pos0 // tq
    cq_per_b = skv // tq
    wblk = heads * LANES
    kv_ops, kv_specs, scratch = _kv_specs(k, v, skv=skv, wblk=wblk, heads=heads, n_heads=H_A, cache=cache)
    cache_cfg = None if cache is None else dict(n_heads=H_A, past=cache[3], t_new=k.shape[0] // nb)
    return pl.pallas_call(
        functools.partial(_fox_kernel, q_pos0=q_pos0, tq=tq, tk=tk, heads=heads, single_q=nq == 1, cache=cache_cfg),
        grid=(nb, H_A // heads, nq),
        in_specs=[pl.BlockSpec((tq, wblk), lambda b, h, i: (b * nq + i, h))] + kv_specs + [
            pl.BlockSpec((tq, LANES), lambda b, h, i: (b * cq_per_b + cq_blk0 + i, 0)),
            pl.BlockSpec((None, heads, 1, skv), lambda b, h, i: (b, h, 0, 0)),
        ],
        out_specs=pl.BlockSpec((tq, wblk), lambda b, h, i: (b * nq + i, h)),
        out_shape=jax.ShapeDtypeStruct((nb * sq, W_A), BF16),
        scratch_shapes=scratch,
        compiler_params=_params(("arbitrary", "arbitrary", "arbitrary")),
        name="fox_attn",
    )(q, *kv_ops, cum_col, cum_row)


def _sb_kernel(q_ref, *refs, q_pos0, tq, tk, heads, single_q, cache):
    (u_ref, o_ref), k_get, v_get = _kv_getters(refs, tk=tk, heads=heads, cache=cache)
    qpos0 = q_pos0 if single_q else q_pos0 + pl.program_id(2) * tq
    n_full = qpos0 // tk
    n_all = (qpos0 + tq - 2) // tk + 1
    scale = HEAD_DIM ** -0.5
    u = u_ref[...]
    qs = [q_ref[:, gi * LANES:(gi + 1) * LANES] for gi in range(heads)]

    def step(r, carry, masked, j_top):
        j = j_top - r
        off = _block_offset(j, tk, single_q)
        out = []
        for gi in range(heads):
            rem_run, acc = carry[gi]
            kj = k_get(gi, off)
            vj = v_get(gi, off)
            z = lax.dot_general(qs[gi], kj, _NT, preferred_element_type=F32) * scale
            ls = _log_sigmoid(z)
            l1m = ls - z
            if masked:
                qp = qpos0 + lax.broadcasted_iota(jnp.int32, (tq, tk), 0)
                kp = off + lax.broadcasted_iota(jnp.int32, (tq, tk), 1)
                mask = kp < qp
                l1m = jnp.where(mask, l1m, 0.0)
            hi = l1m.astype(BF16)
            lo = (l1m - hi.astype(F32)).astype(BF16)
            rem = (jnp.dot(hi, u, preferred_element_type=F32) + jnp.dot(lo, u, preferred_element_type=F32)
                   + rem_run)
            w = jnp.exp(ls + rem)
            if masked:
                w = jnp.where(mask, w, 0.0)
            acc = acc + jnp.dot(w.astype(BF16), vj, preferred_element_type=F32)
            rem_run = rem_run + jnp.sum(l1m, axis=-1, keepdims=True)
            out.append((rem_run, acc))
        return tuple(out)

    init = tuple((jnp.zeros((tq, 1), F32), jnp.zeros((tq, HEAD_DIM), F32)) for _ in range(heads))
    carry = _sweep(0, n_all - n_full, functools.partial(step, masked=True, j_top=n_all - 1), init, single_q)
    carry = _sweep(0, n_full, functools.partial(step, masked=False, j_top=n_full - 1), carry, single_q)
    for gi in range(heads):
        o_ref[:, gi * LANES:(gi + 1) * LANES] = carry[gi][1].astype(o_ref.dtype)


def _sb(q, k, v, *, nb, sq, skv, q_pos0, tq, tk, heads, cache=None):
    nq = sq // tq
    wblk = heads * LANES
    r = lax.broadcasted_iota(jnp.int32, (tk, tk), 0)
    c = lax.broadcasted_iota(jnp.int32, (tk, tk), 1)
    u = (r > c).astype(BF16)
    kv_ops, kv_specs, scratch = _kv_specs(k, v, skv=skv, wblk=wblk, heads=heads, n_heads=H_C, cache=cache)
    cache_cfg = None if cache is None else dict(n_heads=H_C, past=cache[3], t_new=k.shape[0] // nb)
    return pl.pallas_call(
        functools.partial(_sb_kernel, q_pos0=q_pos0, tq=tq, tk=tk, heads=heads, single_q=nq == 1, cache=cache_cfg),
        grid=(nb, H_C // heads, nq),
        in_specs=[pl.BlockSpec((tq, wblk), lambda b, h, i: (b * nq + i, h))] + kv_specs + [
            pl.BlockSpec((tk, tk), lambda b, h, i: (0, 0)),
        ],
        out_specs=pl.BlockSpec((tq, wblk), lambda b, h, i: (b * nq + i, h)),
        out_shape=jax.ShapeDtypeStruct((nb * sq, W_C), BF16),
        scratch_shapes=scratch,
        compiler_params=_params(("arbitrary", "arbitrary", "arbitrary")),
        name="sb_attn",
    )(q, *kv_ops, u)


def _tile_rows(idx, n_tiles, tile):
    return pl.ds(pl.multiple_of(jnp.minimum(idx, n_tiles - 1) * tile, tile), tile)


def _sb_pipe_kernel(q_ref, k_ref, v_ref, u_ref, o_ref, z_ref, ls_ref, hi_ref, lo_ref, w_ref, *, nq, tq, heads):
    n_pairs = nq * (nq + 1) // 2
    scale = HEAD_DIM ** -0.5
    u = u_ref[...]
    for ref in (z_ref, ls_ref, hi_ref, lo_ref, w_ref):
        ref[...] = jnp.zeros(ref.shape, ref.dtype)
    col_minus_row = (lax.broadcasted_iota(jnp.int32, (tq, tq), 1) - lax.broadcasted_iota(jnp.int32, (tq, tq), 0))

    def rows(idx):
        return _tile_rows(idx, nq, tq)

    def body(t, carry):
        (i1, j1, i2, j2, i3, j3, i4, j4), vec = carry
        new_vec = []
        for gi in range(heads):
            cs = slice(gi * LANES, (gi + 1) * LANES)
            acc, run, tot = vec[gi]
            acc = jnp.where(j4 == i4, 0.0, acc) + jnp.dot(w_ref[gi], v_ref[rows(j4), cs], preferred_element_type=F32)
            o_ref[rows(i4), cs] = acc.astype(o_ref.dtype)
            run = jnp.where(j3 == i3, 0.0, run)
            rem = (jnp.dot(hi_ref[gi], u, preferred_element_type=F32)
                   + jnp.dot(lo_ref[gi], u, preferred_element_type=F32) + run)
            w_ref[gi] = jnp.exp(ls_ref[gi] + rem).astype(BF16)
            run = run + tot
            thr = (jnp.minimum(i2, nq - 1) - jnp.minimum(j2, nq - 1)) * tq
            z = jnp.where(col_minus_row < thr, z_ref[gi], NEG_BIG)
            ls = _log_sigmoid(z)
            l1m = ls - z
            hi = l1m.astype(BF16)
            ls_ref[gi] = ls
            hi_ref[gi] = hi
            lo_ref[gi] = (l1m - hi.astype(F32)).astype(BF16)
            tot = jnp.sum(l1m, axis=-1, keepdims=True)
            z_ref[gi] = lax.dot_general(q_ref[rows(i1), cs], k_ref[rows(j1), cs], _NT,
                                        preferred_element_type=F32) * scale
            new_vec.append((acc, run, tot))
        done = j1 == 0
        nxt = (jnp.where(done, i1 + 1, i1), jnp.where(done, i1 + 1, j1 - 1))
        return (nxt + (i1, j1, i2, j2, i3, j3)), tuple(new_vec)

    zero = jnp.int32(0)
    zero_col = jnp.sum(z_ref[0], axis=-1, keepdims=True)
    vec0 = tuple((jnp.zeros((tq, HEAD_DIM), F32), zero_col, zero_col) for _ in range(heads))
    lax.fori_loop(0, n_pairs + 3, body, ((zero,) * 8, vec0))


def _sb_prompt(q, k, v, *, nb, s, tq, heads):
    wblk = heads * LANES
    r = lax.broadcasted_iota(jnp.int32, (tq, tq), 0)
    c = lax.broadcasted_iota(jnp.int32, (tq, tq), 1)
    u = (r > c).astype(BF16)
    seq = pl.BlockSpec((s, wblk), lambda b, h: (b, h))
    return pl.pallas_call(
        functools.partial(_sb_pipe_kernel, nq=s // tq, tq=tq, heads=heads),
        grid=(nb, H_C // heads),
        in_specs=[seq, seq, seq, pl.BlockSpec((tq, tq), lambda b, h: (0, 0))],
        out_specs=seq,
        out_shape=jax.ShapeDtypeStruct((nb * s, W_C), BF16),
        scratch_shapes=[pltpu.VMEM((heads, tq, tq), F32), pltpu.VMEM((heads, tq, tq), F32),
                        pltpu.VMEM((heads, tq, tq), BF16), pltpu.VMEM((heads, tq, tq), BF16),
                        pltpu.VMEM((heads, tq, tq), BF16)],
        compiler_params=_params(("arbitrary", "arbitrary")),
        name="sb_attn_p",
    )(q, k, v, u)


def _fox_pipe_kernel(q_ref, k_ref, v_ref, cq_ref, ck_ref, o_ref, s_ref, p_ref, *, nq, tq, heads):
    n_pairs = nq * (nq + 1) // 2
    hg = pl.program_id(1)
    scale = HEAD_DIM ** -0.5
    s_ref[...] = jnp.zeros(s_ref.shape, s_ref.dtype)
    p_ref[...] = jnp.zeros(p_ref.shape, p_ref.dtype)
    col_minus_row = (lax.broadcasted_iota(jnp.int32, (tq, tq), 1) - lax.broadcasted_iota(jnp.int32, (tq, tq), 0))
    lane = lax.broadcasted_iota(jnp.int32, (tq, LANES), 1)

    def rows(idx):
        return _tile_rows(idx, nq, tq)

    def body(t, carry):
        (i1, j1, i2, j2, i3, j3), vec = carry
        new_vec = []
        cq_all = cq_ref[rows(i2), :]
        for gi in range(heads):
            cs = slice(gi * LANES, (gi + 1) * LANES)
            acc, m, l, alpha_p, l_p = vec[gi]
            acc = alpha_p * acc + jnp.dot(p_ref[gi], v_ref[rows(j3), cs], preferred_element_type=F32)
            o_ref[rows(i3), cs] = (acc / l_p).astype(o_ref.dtype)
            first = j2 == 0
            m = jnp.where(first, NEG_BIG, m)
            l = jnp.where(first, 0.0, l)
            cq = jnp.sum(jnp.where(lane == hg * heads + gi, cq_all, 0.0), axis=-1, keepdims=True)
            sc = s_ref[gi] + (cq - ck_ref[gi, :, rows(j2)])
            thr = (jnp.minimum(i2, nq - 1) - jnp.minimum(j2, nq - 1)) * tq
            sc = jnp.where(col_minus_row <= thr, sc, -jnp.inf)
            m_new = jnp.maximum(m, jnp.max(sc, axis=-1, keepdims=True))
            alpha_p = jnp.exp(m - m_new)
            p = jnp.exp(sc - m_new)
            l = alpha_p * l + jnp.sum(p, axis=-1, keepdims=True)
            l_p = l
            p_ref[gi] = p.astype(BF16)
            s_ref[gi] = lax.dot_general(q_ref[rows(i1), cs], k_ref[rows(j1), cs], _NT,
                                        preferred_element_type=F32) * scale
            new_vec.append((acc, m_new, l, alpha_p, l_p))
        done = j1 == i1
        nxt = (jnp.where(done, i1 + 1, i1), jnp.where(done, 0, j1 + 1))
        return (nxt + (i1, j1, i2, j2)), tuple(new_vec)

    zero = jnp.int32(0)
    zero_col = jnp.sum(s_ref[0], axis=-1, keepdims=True)
    vec0 = tuple((jnp.zeros((tq, HEAD_DIM), F32), zero_col + NEG_BIG, zero_col, zero_col, zero_col + 1.0)
                 for _ in range(heads))
    lax.fori_loop(0, n_pairs + 2, body, ((zero,) * 6, vec0))


def _fox_prompt(q, k, v, cum_col, cum_row, *, nb, s, tq, heads):
    wblk = heads * LANES
    seq = pl.BlockSpec((s, wblk), lambda b, h: (b, h))
    return pl.pallas_call(
        functools.partial(_fox_pipe_kernel, nq=s // tq, tq=tq, heads=heads),
        grid=(nb, H_A // heads),
        in_specs=[seq, seq, seq,
                  pl.BlockSpec((s, LANES), lambda b, h: (b, 0)),
                  pl.BlockSpec((None, heads, 1, s), lambda b, h: (b, h, 0, 0))],
        out_specs=seq,
        out_shape=jax.ShapeDtypeStruct((nb * s, W_A), BF16),
        scratch_shapes=[pltpu.VMEM((heads, tq, tq), F32), pltpu.VMEM((heads, tq, tq), BF16)],
        compiler_params=_params(("arbitrary", "arbitrary")),
        name="fox_attn_p",
    )(q, k, v, cum_col, cum_row)


def _cross_kernel(q_ref, k_ref, v_ref, o_ref):
    q = q_ref[...]
    k = k_ref[...].astype(BF16)
    v = v_ref[...].astype(BF16)
    s = lax.dot_general(q, k, _NT, preferred_element_type=F32) * (HD_X ** -0.5)
    m = jnp.max(s, axis=-1, keepdims=True)
    p = jnp.exp(s - m)
    l = jnp.sum(p, axis=-1, keepdims=True)
    o = jnp.dot(p.astype(BF16), v, preferred_element_type=F32) / l
    o_ref[...] = o.astype(o_ref.dtype)


def _cross(q, mk, mv, *, nb, sq, tq):
    nq = sq // tq
    return pl.pallas_call(
        _cross_kernel,
        grid=(nb, H_X, nq),
        in_specs=[
            pl.BlockSpec((tq, HD_X), lambda b, h, i: (b * nq + i, h)),
            pl.BlockSpec((N_MEM, HD_X), lambda b, h, i: (b, h)),
            pl.BlockSpec((N_MEM, HD_X), lambda b, h, i: (b, h)),
        ],
        out_specs=pl.BlockSpec((tq, HD_X), lambda b, h, i: (b * nq + i, h)),
        out_shape=jax.ShapeDtypeStruct((nb * sq, D_MODEL), BF16),
        compiler_params=_params(("arbitrary", "arbitrary", "arbitrary")),
        name="cross_attn",
    )(q, mk, mv)


def _hgrn_kernel(q_ref, f_ref, i_ref, g_ref, lb_ref, ng_ref, s0_ref, o_ref, sf_ref, st_ref, *, heads, ts):
    t = pl.program_id(2)
    lbk = HGRN_BLOCK

    @pl.when(t == 0)
    def _():
        for gi in range(heads):
            st_ref[gi] = s0_ref[gi].T

    row = lax.broadcasted_iota(jnp.int32, (lbk, LANES), 0)

    def block(r, carry):
        off = pl.multiple_of(r * lbk, lbk)
        for gi in range(heads):
            cs = slice(gi * LANES, (gi + 1) * LANES)
            xq = q_ref[pl.ds(off, lbk), cs]
            xf = f_ref[pl.ds(off, lbk), cs]
            xi = i_ref[pl.ds(off, lbk), cs]
            xg = g_ref[pl.ds(off, lbk), cs]
            lb = lb_ref[:, cs]
            f = lb + (1.0 - lb) * _sigmoid(xf)
            kk = 1.0 - f
            qq = xq * _sigmoid(xq)
            bcum = jnp.log(f)
            sh = 1
            while sh < lbk:
                bcum = bcum + jnp.where(row >= sh, pltpu.roll(bcum, sh, 0), 0.0)
                sh *= 2
            o = jnp.sum(qq * kk, axis=-1, keepdims=True) * xi
            ckey = bcum - jnp.log(kk)
            for d in range(1, lbk):
                e = jnp.exp(bcum - pltpu.roll(ckey, d, 0))
                wgt = jnp.sum(jnp.where(row >= d, qq * e, 0.0), axis=-1, keepdims=True)
                o = o + wgt * pltpu.roll(xi, d, 0)
            st = st_ref[gi]
            qe = (qq * jnp.exp(bcum)).astype(BF16)
            o = o + lax.dot_general(qe, st.astype(BF16), _NT, preferred_element_type=F32)
            b_last = bcum[lbk - 1:lbk, :]
            ke = (kk * jnp.exp(b_last - bcum)).astype(BF16)
            st_ref[gi] = st * jnp.exp(b_last) + lax.dot_general(xi.astype(BF16), ke, _TN, preferred_element_type=F32)
            rms = lax.rsqrt(jnp.mean(o * o, axis=-1, keepdims=True) + RMS_EPS)
            gated = o * rms * ng_ref[:, cs] * (xg * _sigmoid(xg))
            o_ref[pl.ds(off, lbk), cs] = gated.astype(o_ref.dtype)
        return carry

    lax.fori_loop(0, ts // lbk, block, 0)

    @pl.when(t == pl.num_programs(2) - 1)
    def _():
        for gi in range(heads):
            sf_ref[gi] = st_ref[gi].T


def _hgrn(hqf, hig, lb, ng, s0, *, nb, s, ts, heads):
    nt = s // ts
    ng_groups = H_B // heads
    wblk = heads * LANES

    def col(seg):
        return lambda b, hg, t: (b * nt + t, seg * ng_groups + hg)

    return pl.pallas_call(
        functools.partial(_hgrn_kernel, heads=heads, ts=ts),
        grid=(nb, ng_groups, nt),
        in_specs=[
            pl.BlockSpec((ts, wblk), col(0)),
            pl.BlockSpec((ts, wblk), col(1)),
            pl.BlockSpec((ts, wblk), col(0)),
            pl.BlockSpec((ts, wblk), col(1)),
            pl.BlockSpec((1, wblk), lambda b, hg, t: (0, hg)),
            pl.BlockSpec((1, wblk), lambda b, hg, t: (0, hg)),
            pl.BlockSpec((None, heads, LANES, LANES), lambda b, hg, t: (b, hg, 0, 0)),
        ],
        out_specs=[
            pl.BlockSpec((ts, wblk), lambda b, hg, t: (b * nt + t, hg)),
            pl.BlockSpec((None, heads, LANES, LANES), lambda b, hg, t: (b, hg, 0, 0)),
        ],
        out_shape=[
            jax.ShapeDtypeStruct((nb * s, W_B), BF16),
            jax.ShapeDtypeStruct((nb, H_B, LANES, LANES), F32),
        ],
        scratch_shapes=[pltpu.VMEM((heads, LANES, LANES), F32)],
        compiler_params=_params(("arbitrary", "arbitrary", "arbitrary")),
        name="hgrn",
    )(hqf, hqf, hig, hig, lb, ng, s0)


def kernel(x_prompt, x_sample, mem_prompt, cache_fox_k, cache_fox_v, cache_fox_logf, state_hgrn, cache_sb_k, cache_sb_v, cache_mem_k, cache_mem_v, ln_g, ln_b, ffn1_w_gate, ffn1_w_up, ffn1_w_down, ffn2_w_gate, ffn2_w_up, ffn2_w_down, x_w_q, x_w_kv, x_w_o, ev_w_in, fox_b_f, hgrn_lb_logits, hgrn_norm_g, ev_w_out, od_w_in, od_w_out):
    nbp, sp, _ = x_prompt.shape
    nbs, ss, _ = x_sample.shape
    past = cache_fox_k.shape[2]
    s_all = ((past + ss + SAMPLE_TK - 1) // SAMPLE_TK) * SAMPLE_TK
    tm_p, tm_s = TM_PROMPT, nbs * ss
    xp = x_prompt.reshape(nbp * sp, D_MODEL)
    xs = x_sample.reshape(nbs * ss, D_MODEL)
    mem = mem_prompt.reshape(nbp * N_MEM, D_MODEL)

    outs = {k: [] for k in ("fk_p", "fv_p", "flf_p", "hs_p", "sk_p", "sv_p", "mk_p", "mv_p",
                            "fk_s", "fv_s", "flf_s", "hs_s", "sk_s", "sv_s")}
    w1 = (ffn1_w_gate.astype(BF16), ffn1_w_up.astype(BF16), ffn1_w_down.astype(BF16))
    w2 = (ffn2_w_gate.astype(BF16), ffn2_w_up.astype(BF16), ffn2_w_down.astype(BF16))
    for l in range(DEPTH):
        g = [ln_g[l, k].reshape(1, D_MODEL) for k in range(4)]
        bb = [ln_b[l, k].reshape(1, D_MODEL) for k in range(4)]
        j = l // 2

        xp = _ffn(xp, *w1, l, g[0], bb[0], tm=tm_p)
        xs = _ffn(xs, *w1, l, g[0], bb[0], tm=tm_s)

        if l % 2 == 0:
            w_in = ev_w_in
            w_hb = ev_w_in[:, :, 3 * W_A + H_A:]
            bf_pad = jnp.pad(fox_b_f[j].astype(F32), (0, LANES - H_A)).reshape(1, LANES)
            lb = jnp.cumsum(jax.nn.softmax(hgrn_lb_logits.astype(F32), axis=0), axis=0)[j].reshape(1, W_B)
            ng = hgrn_norm_g[j].astype(F32).reshape(1, W_B)
            w_out = ev_w_out

            def even_proj(x, tm, tag):
                qa = _proj(x, w_in, j, col=0, tn=W_A, tm=tm, kind="bf16", name="ev_q" + tag)
                ka16, ka = _proj(x, w_in, j, col=1, tn=W_A, tm=tm, kind="kv", name="ev_k" + tag)
                va16, va = _proj(x, w_in, j, col=2, tn=W_A, tm=tm, kind="kv", name="ev_v" + tag)
                hqf = _proj(x, w_hb, j, col=0, tn=2 * W_B, tm=tm, kind="f32", name="ev_qf" + tag)
                hig = _proj(x, w_hb, j, col=1, tn=2 * W_B, tm=tm, kind="f32", name="ev_ig" + tag)
                lf = _logf(x, w_in, j, bf_pad, tm=tm)
                return qa, ka16, ka, va16, va, hqf, hig, lf

            qa, ka16, ka, va16, va, hqf, hig, lf = even_proj(xp, tm_p, "")
            cum_col, cum_row = _cum(lf, nb=nbp, skv=sp)
            oa = _fox_prompt(qa, ka16, va16, cum_col, cum_row.reshape(nbp, 8, 1, sp), nb=nbp, s=sp,
                             tq=ATTN_TQ, heads=ATTN_HEADS)
            ob, s_fin = _hgrn(hqf, hig, lb, ng, jnp.zeros((nbp, H_B, LANES, LANES), F32), nb=nbp, s=sp,
                              ts=HGRN_TS, heads=HGRN_HEADS)
            outs["fk_p"].append(ka.reshape(nbp, sp, H_A, HEAD_DIM))
            outs["fv_p"].append(va.reshape(nbp, sp, H_A, HEAD_DIM))
            outs["flf_p"].append(lf[:, :H_A].reshape(nbp, sp, H_A))
            outs["hs_p"].append(s_fin)
            xp = _lin_ln([oa, ob], w_out, j, xp, g[1], bb[1], tm=tm_p, name="ev_out_ln")

            qa, ka16, ka, va16, va, hqf, hig, lf = even_proj(xs, tm_s, "_s")
            c_lf = jnp.pad(cache_fox_logf[j].astype(F32), ((0, 0), (0, 0), (0, LANES - H_A)))
            lf_all = jnp.concatenate(
                [c_lf, lf.reshape(nbs, ss, LANES), jnp.zeros((nbs, s_all - past - ss, LANES), F32)], axis=1)
            cum_col, cum_row = _cum(lf_all.reshape(nbs * s_all, LANES), nb=nbs, skv=s_all)
            oa = _fox(qa, ka16, va16, cum_col, cum_row.reshape(nbs, 8, 1, s_all), nb=nbs, sq=ss, skv=s_all,
                      q_pos0=past, tq=ss, tk=SAMPLE_TK, heads=H_A, cache=(cache_fox_k, cache_fox_v, j, past))
            ob, s_new = _hgrn(hqf, hig, lb, ng, state_hgrn[j].astype(F32), nb=nbs, s=ss, ts=ss, heads=HGRN_HEADS)
            outs["fk_s"].append(ka.reshape(nbs, ss, H_A, HEAD_DIM))
            outs["fv_s"].append(va.reshape(nbs, ss, H_A, HEAD_DIM))
            outs["flf_s"].append(lf[:, :H_A].reshape(nbs, ss, H_A))
            outs["hs_s"].append(s_new)
            xs = _lin_ln([oa, ob], w_out, j, xs, g[1], bb[1], tm=tm_s, name="ev_out_ln_s")
        else:
            w_in = od_w_in
            w_out = od_w_out

            def odd_proj(x, tm, tag):
                q = _proj(x, w_in, j, col=0, tn=W_C, tm=tm, kind="bf16", name="od_q" + tag)
                k16, k = _proj(x, w_in, j, col=1, tn=W_C, tm=tm, kind="kv", name="od_k" + tag)
                v16, v = _proj(x, w_in, j, col=2, tn=W_C, tm=tm, kind="kv", name="od_v" + tag)
                return q, k16, k, v16, v

            q, k16, k, v16, v = odd_proj(xp, tm_p, "")
            o = _sb_prompt(q, k16, v16, nb=nbp, s=sp, tq=ATTN_TQ, heads=ATTN_HEADS)
            outs["sk_p"].append(k.reshape(nbp, sp, H_C, HEAD_DIM))
            outs["sv_p"].append(v.reshape(nbp, sp, H_C, HEAD_DIM))
            xp = _lin_ln([o], w_out, j, xp, g[1], bb[1], tm=tm_p, name="od_out_ln")

            q, k16, k, v16, v = odd_proj(xs, tm_s, "_s")
            o = _sb(q, k16, v16, nb=nbs, sq=ss, skv=s_all, q_pos0=past, tq=ss, tk=LANES, heads=H_C // 2,
                    cache=(cache_sb_k, cache_sb_v, j, past))
            outs["sk_s"].append(k.reshape(nbs, ss, H_C, HEAD_DIM))
            outs["sv_s"].append(v.reshape(nbs, ss, H_C, HEAD_DIM))
            xs = _lin_ln([o], w_out, j, xs, g[1], bb[1], tm=tm_s, name="od_out_ln_s")

        mk = _proj(mem, x_w_kv, l, col=0, tn=D_MODEL, tm=TM_PROMPT, kind="f32", name="mem_k")
        mv = _proj(mem, x_w_kv, l, col=1, tn=D_MODEL, tm=TM_PROMPT, kind="f32", name="mem_v")
        outs["mk_p"].append(mk.reshape(nbp, N_MEM, H_X, HD_X))
        outs["mv_p"].append(mv.reshape(nbp, N_MEM, H_X, HD_X))
        qx = _proj(xp, x_w_q, l, col=0, tn=D_MODEL, tm=tm_p, kind="bf16", name="x_q")
        ox = _cross(qx, mk, mv, nb=nbp, sq=sp, tq=CROSS_TQ)
        xp = _lin_ln([ox], x_w_o, l, xp, g[2], bb[2], tm=tm_p, name="x_out_ln")
        qx = _proj(xs, x_w_q, l, col=0, tn=D_MODEL, tm=tm_s, kind="bf16", name="x_q_s")
        ox = _cross(qx, cache_mem_k[l].reshape(nbs * N_MEM, D_MODEL).astype(F32),
                    cache_mem_v[l].reshape(nbs * N_MEM, D_MODEL).astype(F32), nb=nbs, sq=ss, tq=ss)
        xs = _lin_ln([ox], x_w_o, l, xs, g[2], bb[2], tm=tm_s, name="x_out_ln_s")

        xp = _ffn(xp, *w2, l, g[3], bb[3], tm=tm_p)
        xs = _ffn(xs, *w2, l, g[3], bb[3], tm=tm_s)

    st = {k: jnp.stack(v) for k, v in outs.items()}
    return (xp.reshape(nbp, sp, D_MODEL), xs.reshape(nbs, ss, D_MODEL),
            st["fk_p"], st["fv_p"], st["flf_p"], st["hs_p"], st["sk_p"], st["sv_p"], st["mk_p"], st["mv_p"],
            st["fk_s"], st["fv_s"], st["flf_s"], st["hs_s"], st["sk_s"], st["sv_s"])
```
